```python
import jax, jax.numpy as jnp
from jax import lax
import numpy as np

D_MODEL = 1024
BATCH = 2
SEQ = 16384
DEPTH = 2

GRID_W = 64
CTX_LEN = 256
D_MIX = D_MODEL
D_LRU = D_MIX // 4
LRU_HEADS = 4
LRU_BW = D_LRU // LRU_HEADS
LRU_C = 8.0
CONV_W = 4
CONV_LEFT = 2
D_S5 = D_MIX // 4
S5_H = 16
S5_G = D_S5 // S5_H
S5_P = 64
D_ATT = D_MIX - D_LRU - D_S5
NA_HD = 64
NA_HEADS = D_ATT // NA_HD
NA_KH = 8
NA_KW = 16
NA_QB = NA_KW
NA_KWB = 2 * NA_KW
ROPE_BASE = 10000.0
D_IN = 2 * D_LRU + D_S5 + 3 * D_ATT
N_EXPERTS = 32
TOP_K = 4
D_FF = D_MODEL
SWIGLU_LIMIT = 7.0
SWIGLU_ALPHA = 1.702
MOE_BLOCK = 256
N_MOD = 6
EPS = 1e-6

kernel_name = 'hybrid_rglru_s5_natten_moe_dit'

F32 = jnp.float32


def rmsnorm(x, g):
    xf = x.astype(F32)
    y = xf * lax.rsqrt(jnp.mean(xf * xf, axis=-1, keepdims=True) + EPS)
    return (y * g.astype(F32)).astype(x.dtype)


def modulate(x, g, shift, scale):
    return rmsnorm(x, g) * (1.0 + scale) + shift


def centred_dwconv(x, w, b):
    n = x.shape[1]
    xp = jnp.pad(x, ((0, 0), (CONV_LEFT, CONV_W - 1 - CONV_LEFT), (0, 0)))
    out = b
    for k in range(CONV_W):
        out = out + xp[:, k:k + n] * w[k]
    return out


def _lin_op(e1, e2):
    a1, b1 = e1
    a2, b2 = e2
    return a1 * a2, a2 * b1 + b2


def linear_scan(a, b, h0, reverse):
    a_cum, h = lax.associative_scan(_lin_op, (a, b), axis=1, reverse=reverse)
    if h0 is not None:
        h = h + a_cum * h0[:, None]
    return h


def _cplx_op(e1, e2):
    a1r, a1i, b1r, b1i = e1
    a2r, a2i, b2r, b2i = e2
    return (a1r * a2r - a1i * a2i, a1r * a2i + a1i * a2r,
            a2r * b1r - a2i * b1i + b2r, a2r * b1i + a2i * b1r + b2i)


def complex_scan(ar, ai, br, bi, h0, reverse):
    car, cai, hr, hi = lax.associative_scan(_cplx_op, (ar, ai, br, bi), axis=1, reverse=reverse)
    if h0 is not None:
        h0r, h0i = h0[0][:, None], h0[1][:, None]
        hr = hr + car * h0r - cai * h0i
        hi = hi + car * h0i + cai * h0r
    return hr, hi


def block_diag_linear(x, w, b):
    xb = x.reshape(x.shape[:-1] + (LRU_HEADS, LRU_BW))
    return jnp.einsum('blnc,ncd->blnd', xb, w).reshape(x.shape) + b


def rglru_coeffs(u, w_a, b_a, w_x, b_x, lam):
    uf = u.astype(F32)
    r = jax.nn.sigmoid(block_diag_linear(uf, w_a.astype(F32), b_a.astype(F32)))
    i = jax.nn.sigmoid(block_diag_linear(uf, w_x.astype(F32), b_x.astype(F32)))
    log_a = -LRU_C * r * jax.nn.softplus(-lam.astype(F32))
    a = jnp.exp(log_a)
    b = jnp.sqrt(-jnp.expm1(2.0 * log_a)) * (i * uf)
    return a, b


def rglru_mixer(u_lat, g_lat, u_ctx, g_ctx, conv_w, conv_b, w_a, b_a, w_x, b_x, lam, need_ctx):
    c_lat = centred_dwconv(u_lat, conv_w, conv_b)
    c_ctx = centred_dwconv(u_ctx, conv_w, conv_b)
    y_lat, y_ctx = [], []
    for d, rev in enumerate((False, True)):
        a_c, b_c = rglru_coeffs(c_ctx, w_a[d], b_a[d], w_x[d], b_x[d], lam[d])
        h_c = linear_scan(a_c, b_c, None, rev)
        h_end = h_c[:, 0] if rev else h_c[:, -1]
        a_l, b_l = rglru_coeffs(c_lat, w_a[d], b_a[d], w_x[d], b_x[d], lam[d])
        y_lat.append(linear_scan(a_l, b_l, h_end, rev))
        y_ctx.append(h_c)
    out_lat = (y_lat[0] + y_lat[1]).astype(u_lat.dtype) * jax.nn.gelu(g_lat)
    out_ctx = (y_ctx[0] + y_ctx[1]).astype(u_ctx.dtype) * jax.nn.gelu(g_ctx) if need_ctx else None
    return out_lat, out_ctx


def s5_discretise(a_re, a_im, log_dt, b_re, b_im):
    a_re, a_im = a_re.astype(F32), a_im.astype(F32)
    dt = jnp.exp(log_dt.astype(F32))[:, None]
    mag = jnp.exp(a_re * dt)
    abar_r, abar_i = mag * jnp.cos(a_im * dt), mag * jnp.sin(a_im * dt)
    den = a_re * a_re + a_im * a_im
    nr = abar_r - 1.0
    coef_r = (nr * a_re + abar_i * a_im) / den
    coef_i = (abar_i * a_re - nr * a_im) / den
    br, bi = b_re.astype(F32), b_im.astype(F32)
    bbar_r = coef_r[..., None] * br - coef_i[..., None] * bi
    bbar_i = coef_r[..., None] * bi + coef_i[..., None] * br
    return abar_r, abar_i, bbar_r, bbar_i


def s5_scan(u, abar_r, abar_i, bbar_r, bbar_i, h0, reverse):
    bu_r = jnp.einsum('blgh,gph->blgp', u, bbar_r)
    bu_i = jnp.einsum('blgh,gph->blgp', u, bbar_i)
    ar = jnp.broadcast_to(abar_r, bu_r.shape)
    ai = jnp.broadcast_to(abar_i, bu_r.shape)
    return complex_scan(ar, ai, bu_r, bu_i, h0, reverse)


def s5_readout(hr, hi, c_re, c_im):
    return (jnp.einsum('blgp,ghp->blgh', hr, c_re.astype(F32))
            - jnp.einsum('blgp,ghp->blgh', hi, c_im.astype(F32)))


def s5_mixer(u_lat, u_ctx, a_re, a_im, log_dt, b_re, b_im, c_re, c_im, d_skip, w_glu, b_glu, need_ctx):
    def groups(u):
        return u.astype(F32).reshape(u.shape[0], u.shape[1], S5_G, S5_H)
    ul, uc = groups(u_lat), groups(u_ctx)
    dsk = d_skip.astype(F32).reshape(S5_G, S5_H)
    y_lat = ul * dsk
    y_ctx = uc * dsk
    for d, rev in enumerate((False, True)):
        ab_r, ab_i, bb_r, bb_i = s5_discretise(a_re[d], a_im[d], log_dt[d], b_re[d], b_im[d])
        hc_r, hc_i = s5_scan(uc, ab_r, ab_i, bb_r, bb_i, None, rev)
        end = 0 if rev else -1
        hl_r, hl_i = s5_scan(ul, ab_r, ab_i, bb_r, bb_i, (hc_r[:, end], hc_i[:, end]), rev)
        y_lat = y_lat + s5_readout(hl_r, hl_i, c_re[d], c_im[d])
        if need_ctx:
            y_ctx = y_ctx + s5_readout(hc_r, hc_i, c_re[d], c_im[d])

    def glu(y, dtype):
        z = jax.nn.gelu(y.reshape(y.shape[0], y.shape[1], D_S5)).astype(dtype)
        return z * jax.nn.sigmoid(z @ w_glu + b_glu)
    out_lat = glu(y_lat, u_lat.dtype)
    out_ctx = glu(y_ctx, u_ctx.dtype) if need_ctx else None
    return out_lat, out_ctx


def axial_rope(t):
    rows, cols = t.shape[2], t.shape[3]
    nf = NA_HD // 4
    half = NA_HD // 2
    inv = ROPE_BASE ** (-jnp.arange(nf, dtype=F32) / nf)
    ang_r = jnp.arange(rows, dtype=F32)[:, None, None] * inv
    ang_c = jnp.arange(cols, dtype=F32)[None, :, None] * inv

    def rot(u, ang):
        cos, sin = jnp.cos(ang).astype(u.dtype), jnp.sin(ang).astype(u.dtype)
        u1, u2 = u[..., :nf], u[..., nf:]
        return jnp.concatenate([u1 * cos - u2 * sin, u2 * cos + u1 * sin], axis=-1)
    return jnp.concatenate([rot(t[..., :half], ang_r), rot(t[..., half:], ang_c)], axis=-1)


def neighbourhood_attention(q_lat, k_lat, v_lat, q_ctx, k_ctx, v_ctx, rpb, need_ctx):
    b, n, _ = q_lat.shape
    rows = n // GRID_W
    kh = min(NA_KH, rows)
    ncb = GRID_W // NA_QB
    scale = NA_HD ** -0.5

    def heads(t):
        return t.reshape(b, t.shape[1], NA_HEADS, NA_HD).transpose(0, 2, 1, 3)
    qc, kc, vc = heads(q_ctx), heads(k_ctx), heads(v_ctx)

    def grid(t):
        return heads(t).reshape(b, NA_HEADS, rows, GRID_W, NA_HD)
    qg, kg, vg = grid(q_lat), grid(k_lat), grid(v_lat)
    qg_rot, kg_rot = axial_rope(qg), axial_rope(kg)

    blk_start = jnp.clip(jnp.arange(ncb) * NA_QB - NA_KW // 2, 0, GRID_W - NA_KWB)
    key_col = blk_start[:, None] + jnp.arange(NA_KWB)
    q_col = jnp.arange(GRID_W).reshape(ncb, NA_QB)
    win_start = jnp.clip(q_col - NA_KW // 2, 0, GRID_W - NA_KW)
    kc3 = key_col[:, None, :]
    col_ok = (kc3 >= win_start[..., None]) & (kc3 < win_start[..., None] + NA_KW)
    key_ok = jnp.broadcast_to(col_ok[:, :, None, :], (ncb, NA_QB, kh, NA_KWB)).reshape(ncb, NA_QB, kh * NA_KWB)
    dc = jnp.clip(kc3 - q_col[..., None] + NA_KW - 1, 0, 2 * NA_KW - 2)
    rpb_cols = rpb[:, :, dc]
    n_loc = kh * NA_KWB

    def row_block(args):
        r, q_rot_r, q_r = args
        rs = jnp.clip(r - kh // 2, 0, rows - kh)

        def gather(t):
            blk = lax.dynamic_slice_in_dim(t, rs, kh, axis=2)[:, :, :, key_col]
            return blk.transpose(0, 1, 3, 2, 4, 5).reshape(b, NA_HEADS, ncb, n_loc, NA_HD)
        k_blk, v_blk = gather(kg_rot), gather(vg)
        qr = q_rot_r.reshape(b, NA_HEADS, ncb, NA_QB, NA_HD)
        qp = q_r.reshape(b, NA_HEADS, ncb, NA_QB, NA_HD)
        dr = rs + jnp.arange(kh) - r + NA_KH - 1
        bias = jnp.take(rpb_cols, dr, axis=1).transpose(0, 2, 3, 1, 4).reshape(NA_HEADS, ncb, NA_QB, n_loc)
        s_loc = jnp.einsum('bhnqd,bhnkd->bhnqk', qr, k_blk).astype(F32) * scale + bias.astype(F32)
        s_loc = jnp.where(key_ok, s_loc, -jnp.inf)
        s_ctx = jnp.einsum('bhnqd,bhcd->bhnqc', qp, kc).astype(F32) * scale
        p = jax.nn.softmax(jnp.concatenate([s_loc, s_ctx], axis=-1), axis=-1).astype(v_blk.dtype)
        out = (jnp.einsum('bhnqk,bhnkd->bhnqd', p[..., :n_loc], v_blk)
               + jnp.einsum('bhnqc,bhcd->bhnqd', p[..., n_loc:], vc))
        return out.reshape(b, NA_HEADS, GRID_W, NA_HD)

    xs = (jnp.arange(rows), jnp.moveaxis(qg_rot, 2, 0), jnp.moveaxis(qg, 2, 0))
    out = lax.map(row_block, xs)
    out_lat = out.transpose(1, 0, 3, 2, 4).reshape(b, n, D_ATT)
    out_ctx = None
    if need_ctx:
        s = jnp.einsum('bhqd,bhkd->bhqk', qc, kc).astype(F32) * scale
        p = jax.nn.softmax(s, axis=-1).astype(vc.dtype)
        oc = jnp.einsum('bhqk,bhkd->bhqd', p, vc)
        out_ctx = oc.transpose(0, 2, 1, 3).reshape(b, q_ctx.shape[1], D_ATT)
    return out_lat, out_ctx


def moe_ffn(h, w_router, b_router, w_gu, b_gu, w_down, b_down):
    n, d = h.shape
    logits = h.astype(F32) @ w_router.astype(F32) + b_router.astype(F32)
    top_v, top_i = lax.top_k(logits, TOP_K)
    gates = jax.nn.softmax(top_v, axis=-1)
    n_assign = n * TOP_K
    e_flat = top_i.reshape(-1)
    tok_flat = jnp.arange(n_assign, dtype=jnp.int32) // TOP_K
    order = jnp.argsort(e_flat)
    e_s, tok_s, g_s = e_flat[order], tok_flat[order], gates.reshape(-1)[order]
    counts = jnp.zeros((N_EXPERTS,), jnp.int32).at[e_flat].add(1)
    padded = (counts + MOE_BLOCK - 1) // MOE_BLOCK * MOE_BLOCK
    start = jnp.cumsum(counts) - counts
    p_end = jnp.cumsum(padded)
    p_start = p_end - padded
    dest = p_start[e_s] + jnp.arange(n_assign, dtype=jnp.int32) - start[e_s]
    n_blocks = -(-n_assign // MOE_BLOCK) + N_EXPERTS
    cap = n_blocks * MOE_BLOCK
    tok_buf = jnp.zeros((cap,), jnp.int32).at[dest].set(tok_s)
    g_buf = jnp.zeros((cap,), F32).at[dest].set(g_s)
    blk_e = jnp.minimum(jnp.searchsorted(p_end, jnp.arange(n_blocks, dtype=jnp.int32) * MOE_BLOCK, side='right'),
                        N_EXPERTS - 1)

    def expert_block(args):
        tok, g, e = args
        xb = h[tok]
        gu = xb @ w_gu[e] + b_gu[e]
        glu = jnp.minimum(gu[:, :D_FF], SWIGLU_LIMIT)
        lin = jnp.clip(gu[:, D_FF:], -SWIGLU_LIMIT, SWIGLU_LIMIT)
        act = glu * jax.nn.sigmoid(SWIGLU_ALPHA * glu) * (lin + 1.0)
        return (act @ w_down[e] + b_down[e]) * g[:, None].astype(h.dtype)

    y_buf = lax.map(expert_block, (tok_buf.reshape(n_blocks, MOE_BLOCK), g_buf.reshape(n_blocks, MOE_BLOCK), blk_e))
    return jnp.zeros_like(h).at[tok_buf].add(y_buf.reshape(cap, d))


def setup_inputs(seed: int = 0) -> dict:
    key = jax.random.key(seed)
    ks = iter(jax.random.split(key, 40))

    def nrm(shape, s):
        return jax.random.normal(next(ks), shape, F32) * s
    L = DEPTH
    a_c = jax.random.uniform(next(ks), (L, 2, D_LRU), F32, minval=0.9, maxval=0.999)
    a0 = a_c ** (1.0 / LRU_C)
    return {
        'x': nrm((BATCH, SEQ, D_MODEL), 1.0),
        'c': nrm((BATCH, D_MODEL), 1.0),
        'ctx': nrm((BATCH, CTX_LEN, D_MODEL), 1.0),
        'c_ctx': nrm((D_MODEL,), 1.0),
        'w_mod': nrm((L, D_MODEL, N_MOD * D_MODEL), 0.5 * D_MODEL ** -0.5),
        'b_mod': nrm((L, N_MOD * D_MODEL), 0.01),
        'norm_g': 1.0 + nrm((L, 4, D_MODEL), 0.05),
        'w_in': nrm((L, D_MODEL, D_IN), D_MODEL ** -0.5),
        'w_out': nrm((L, D_MIX, D_MODEL), D_MIX ** -0.5),
        'lru_conv_w': nrm((L, CONV_W, D_LRU), CONV_W ** -0.5),
        'lru_conv_b': nrm((L, D_LRU), 0.01),
        'lru_w_a': nrm((L, 2, LRU_HEADS, LRU_BW, LRU_BW), LRU_BW ** -0.5),
        'lru_b_a': nrm((L, 2, D_LRU), 0.01),
        'lru_w_x': nrm((L, 2, LRU_HEADS, LRU_BW, LRU_BW), LRU_BW ** -0.5),
        'lru_b_x': nrm((L, 2, D_LRU), 0.01),
        'lru_lambda': jnp.log(a0) - jnp.log1p(-a0),
        's5_a_re': -0.5 + nrm((L, 2, S5_G, S5_P), 0.01),
        's5_a_im': jnp.pi * jnp.arange(S5_P, dtype=F32) + nrm((L, 2, S5_G, S5_P), 0.01),
        's5_log_dt': jax.random.uniform(next(ks), (L, 2, S5_G), F32, minval=float(np.log(1e-3)), maxval=float(np.log(1e-1))),
        's5_b_re': nrm((L, 2, S5_G, S5_P, S5_H), (2.0 * S5_H) ** -0.5),
        's5_b_im': nrm((L, 2, S5_G, S5_P, S5_H), (2.0 * S5_H) ** -0.5),
        's5_c_re': nrm((L, 2, S5_G, S5_H, S5_P), (2.0 * S5_P) ** -0.5),
        's5_c_im': nrm((L, 2, S5_G, S5_H, S5_P), (2.0 * S5_P) ** -0.5),
        's5_d': nrm((L, D_S5), 1.0),
        's5_w_glu': nrm((L, D_S5, D_S5), D_S5 ** -0.5),
        's5_b_glu': nrm((L, D_S5), 0.01),
        'na_rpb': nrm((L, NA_HEADS, 2 * NA_KH - 1, 2 * NA_KW - 1), 0.1),
        'moe_w_router': nrm((L, D_MODEL, N_EXPERTS), D_MODEL ** -0.5),
        'moe_b_router': nrm((L, N_EXPERTS), 0.01),
        'moe_w_gu': nrm((L, N_EXPERTS, D_MODEL, 2 * D_FF), D_MODEL ** -0.5),
        'moe_b_gu': nrm((L, N_EXPERTS, 2 * D_FF), 0.01),
        'moe_w_down': nrm((L, N_EXPERTS, D_FF, D_MODEL), D_FF ** -0.5),
        'moe_b_down': nrm((L, N_EXPERTS, D_MODEL), 0.01),
    }


def reference(x, c, ctx, c_ctx, w_mod, b_mod, norm_g, w_in, w_out, lru_conv_w, lru_conv_b, lru_w_a, lru_b_a,
              lru_w_x, lru_b_x, lru_lambda, s5_a_re, s5_a_im, s5_log_dt, s5_b_re, s5_b_im, s5_c_re, s5_c_im,
              s5_d, s5_w_glu, s5_b_glu, na_rpb, moe_w_router, moe_b_router, moe_w_gu, moe_b_gu, moe_w_down,
              moe_b_down):
    b = x.shape[0]
    xc = ctx
    silu_c = jax.nn.silu(c)
    silu_cc = jax.nn.silu(c_ctx)
    split_at = [D_LRU, 2 * D_LRU, 2 * D_LRU + D_S5, 2 * D_LRU + D_S5 + D_ATT, 2 * D_LRU + D_S5 + 2 * D_ATT]
    for l in range(DEPTH):
        need_ctx = l < DEPTH - 1
        mod_lat = (silu_c @ w_mod[l] + b_mod[l]).reshape(b, N_MOD, 1, D_MODEL)
        mod_ctx = (silu_cc @ w_mod[l] + b_mod[l]).reshape(N_MOD, 1, D_MODEL)
        g = norm_g[l]
        h_lat = modulate(x, g[0], mod_lat[:, 0], mod_lat[:, 1])
        h_ctx = modulate(xc, g[0], mod_ctx[0], mod_ctx[1])
        lu_l, lg_l, su_l, q_l, k_l, v_l = jnp.split(h_lat @ w_in[l], split_at, axis=-1)
        lu_c, lg_c, su_c, q_c, k_c, v_c = jnp.split(h_ctx @ w_in[l], split_at, axis=-1)
        y_lru_l, y_lru_c = rglru_mixer(lu_l, lg_l, lu_c, lg_c, lru_conv_w[l], lru_conv_b[l], lru_w_a[l],
                                       lru_b_a[l], lru_w_x[l], lru_b_x[l], lru_lambda[l], need_ctx)
        y_s5_l, y_s5_c = s5_mixer(su_l, su_c, s5_a_re[l], s5_a_im[l], s5_log_dt[l], s5_b_re[l], s5_b_im[l],
                                  s5_c_re[l], s5_c_im[l], s5_d[l], s5_w_glu[l], s5_b_glu[l], need_ctx)
        y_na_l, y_na_c = neighbourhood_attention(q_l, k_l, v_l, q_c, k_c, v_c, na_rpb[l], need_ctx)
        mix_l = jnp.concatenate([y_lru_l, y_s5_l, y_na_l], axis=-1) @ w_out[l]
        x = x + mod_lat[:, 2] * rmsnorm(mix_l, g[1])
        if need_ctx:
            mix_c = jnp.concatenate([y_lru_c, y_s5_c, y_na_c], axis=-1) @ w_out[l]
            xc = xc + mod_ctx[2] * rmsnorm(mix_c, g[1])
        f_lat = modulate(x, g[2], mod_lat[:, 3], mod_lat[:, 4]).reshape(-1, D_MODEL)
        n_lat = f_lat.shape[0]
        if need_ctx:
            f_ctx = modulate(xc, g[2], mod_ctx[3], mod_ctx[4]).reshape(-1, D_MODEL)
            tokens = jnp.concatenate([f_lat, f_ctx], axis=0)
        else:
            tokens = f_lat
        y = moe_ffn(tokens, moe_w_router[l], moe_b_router[l], moe_w_gu[l], moe_b_gu[l], moe_w_down[l], moe_b_down[l])
        x = x + mod_lat[:, 5] * rmsnorm(y[:n_lat].reshape(x.shape), g[3])
        if need_ctx:
            xc = xc + mod_ctx[5] * rmsnorm(y[n_lat:].reshape(xc.shape), g[3])
    return x
```

```python
import functools

import jax
import jax.numpy as jnp
from jax import lax
from jax.experimental import pallas as pl
from jax.experimental.pallas import tpu as pltpu

F32 = jnp.float32
BF16 = jnp.bfloat16

EPS = 1e-6
N_MOD = 6
LRU_C = 8.0
CONV_W = 4
CONV_LEFT = 2
GRID_W = 64
NA_HD = 64
NA_KH = 8
NA_KW = 16
ROPE_BASE = 10000.0
TOP_K = 4
SWIGLU_LIMIT = 7.0
SWIGLU_ALPHA = 1.702

LANES = 128
SUBLANES = 8
VMEM_LIMIT_BYTES = 56 * 1024 * 1024

NEG_BIG = -1e30


def _cparams(sem):
    return pltpu.CompilerParams(dimension_semantics=sem, vmem_limit_bytes=VMEM_LIMIT_BYTES)


def _rms(x):
    return x * lax.rsqrt(jnp.mean(x * x, axis=-1, keepdims=True) + EPS)


def _mod_kernel(c_ref, w_ref, b_ref, o_ref):
    c = c_ref[...]
    s = c * jax.nn.sigmoid(c)
    o_ref[...] = jnp.dot(s, w_ref[...], preferred_element_type=F32, precision=lax.Precision.HIGHEST) + b_ref[...]


def _modulation(cvec, w_mod, b_mod):
    n_layers, d, nd = w_mod.shape
    tn = nd // 4
    return pl.pallas_call(
        _mod_kernel,
        grid=(n_layers, nd // tn),
        in_specs=[
            pl.BlockSpec((SUBLANES, d), lambda l, j: (0, 0)),
            pl.BlockSpec((None, d, tn), lambda l, j: (l, 0, j)),
            pl.BlockSpec((None, 1, tn), lambda l, j: (l, 0, j)),
        ],
        out_specs=pl.BlockSpec((None, SUBLANES, tn), lambda l, j: (l, 0, j)),
        out_shape=jax.ShapeDtypeStruct((n_layers, SUBLANES, nd), F32),
        compiler_params=_cparams(("arbitrary", "arbitrary")),
        name="modulation",
    )(cvec, w_mod, b_mod.reshape(n_layers, 1, nd))


def _swap16(x):
    w = x.shape[-1]
    lane = lax.broadcasted_iota(jnp.int32, x.shape, 1)
    return jnp.where(lane % 32 < 16, pltpu.roll(x, w - 16, 1), pltpu.roll(x, 16, 1))


def _inproj_kernel(d_lru, d_s5, d_att, x_ref, mod_ref, g_ref, w_ref, cos_ref, sin_ref,
                   lu_ref, lg_ref, su_ref, q_ref, qr_ref, kr_ref, v_ref):
    h = _rms(x_ref[...]) * g_ref[...]
    h = h * (1.0 + mod_ref[1:2, :]) + mod_ref[0:1, :]
    y = jnp.dot(h.astype(BF16), w_ref[...], preferred_element_type=F32)
    o = 0
    lu_ref[...] = y[:, o:o + d_lru]
    o += d_lru
    lg_ref[...] = y[:, o:o + d_lru]
    o += d_lru
    su_ref[...] = y[:, o:o + d_s5]
    o += d_s5
    q = y[:, o:o + d_att]
    o += d_att
    k = y[:, o:o + d_att]
    o += d_att
    v = y[:, o:o + d_att]
    reps = d_att // LANES
    cos = jnp.concatenate([cos_ref[...]] * reps, axis=1)
    sin = jnp.concatenate([sin_ref[...]] * reps, axis=1)
    q_ref[...] = q.astype(BF16)
    qr_ref[...] = (q * cos + _swap16(q) * sin).astype(BF16)
    kr_ref[...] = (k * cos + _swap16(k) * sin).astype(BF16)
    v_ref[...] = v.astype(BF16)


ROW_TILE = 512


def _row_maps(bsz, seq, tm):
    assert seq % tm == 0
    tps = seq // tm
    n_lat = bsz * tps
    mod_row = lambda i: jnp.where(i < n_lat, i // tps, bsz)
    rope_tile = lambda i: jnp.where(i < n_lat, i % tps, tps)
    return mod_row, rope_tile


def _inproj(x2, modl, g0, w_in_bf, cos_t, sin_t, dims, bsz, seq):
    d_lru, d_s5, d_att = dims
    r, d = x2.shape
    d_in = w_in_bf.shape[1]
    tm = ROW_TILE
    assert r % tm == 0
    mod_row, rope_tile = _row_maps(bsz, seq, tm)
    row = lambda i: (i, 0)
    outs = [jax.ShapeDtypeStruct((r, d_lru), F32)] * 2 + [jax.ShapeDtypeStruct((r, d_s5), F32)] + \
           [jax.ShapeDtypeStruct((r, d_att), BF16)] * 4
    return pl.pallas_call(
        functools.partial(_inproj_kernel, d_lru, d_s5, d_att),
        grid=(r // tm,),
        in_specs=[
            pl.BlockSpec((tm, d), row),
            pl.BlockSpec((None, N_MOD, d), lambda i: (mod_row(i), 0, 0)),
            pl.BlockSpec((1, d), lambda i: (0, 0)),
            pl.BlockSpec((d, d_in), lambda i: (0, 0)),
            pl.BlockSpec((tm, LANES), lambda i: (rope_tile(i), 0)),
            pl.BlockSpec((tm, LANES), lambda i: (rope_tile(i), 0)),
        ],
        out_specs=[pl.BlockSpec((tm, d_lru), row)] * 2 + [pl.BlockSpec((tm, d_s5), row)] +
                  [pl.BlockSpec((tm, d_att), row)] * 4,
        out_shape=outs,
        compiler_params=_cparams(("arbitrary",)),
        name="inproj",
    )(x2, modl, g0, w_in_bf, cos_t, sin_t)


def _rope_tables(seq):
    nf = NA_HD // 4
    inv = ROPE_BASE ** (-jnp.arange(nf, dtype=F32) / nf)
    t = jnp.arange(seq)
    ang_r = (t // GRID_W).astype(F32)[:, None] * inv
    ang_c = (t % GRID_W).astype(F32)[:, None] * inv
    cos = jnp.concatenate([jnp.cos(ang_r)] * 2 + [jnp.cos(ang_c)] * 2, axis=1)
    sin = jnp.concatenate([-jnp.sin(ang_r), jnp.sin(ang_r), -jnp.sin(ang_c), jnp.sin(ang_c)], axis=1)
    cos = jnp.concatenate([cos, jnp.ones((ROW_TILE, NA_HD), F32)], axis=0)
    sin = jnp.concatenate([sin, jnp.zeros((ROW_TILE, NA_HD), F32)], axis=0)
    return jnp.concatenate([cos, cos], axis=1), jnp.concatenate([sin, sin], axis=1)


def _outproj_kernel(n_lat_tiles, has_ctx, n_exp, *refs):
    if has_ctx:
        (h0_ref, h1_ref, lg_ref, ys_ref, na_ref, h0c_ref, h1c_ref, ysc_ref, nac_ref,
         x_ref, mod_ref, g_ref, wo_ref, wglu_ref, bglu_ref, wr_ref, br_ref, xo_ref, f_ref, ti_ref, tg_ref) = refs
        is_ctx = pl.program_id(0) >= n_lat_tiles
        pick = lambda a, c: jnp.where(is_ctx, c[...], a[...])
        h0, h1, ys, na = pick(h0_ref, h0c_ref), pick(h1_ref, h1c_ref), pick(ys_ref, ysc_ref), pick(na_ref, nac_ref)
    else:
        (h0_ref, h1_ref, lg_ref, ys_ref, na_ref,
         x_ref, mod_ref, g_ref, wo_ref, wglu_ref, bglu_ref, wr_ref, br_ref, xo_ref, f_ref, ti_ref, tg_ref) = refs
        h0, h1, ys, na = h0_ref[...], h1_ref[...], ys_ref[...], na_ref[...]
    d_lru, d_s5 = h0.shape[1], ys.shape[1]
    y_lru = (h0 + h1) * jax.nn.gelu(lg_ref[...])
    z = jax.nn.gelu(ys)
    y_s5 = z * jax.nn.sigmoid(jnp.dot(z.astype(BF16), wglu_ref[...], preferred_element_type=F32) + bglu_ref[...])
    mix = jnp.dot(y_lru.astype(BF16), wo_ref[0:d_lru, :], preferred_element_type=F32)
    mix = mix + jnp.dot(y_s5.astype(BF16), wo_ref[d_lru:d_lru + d_s5, :], preferred_element_type=F32)
    mix = mix + jnp.dot(na, wo_ref[d_lru + d_s5:, :], preferred_element_type=F32)
    x_new = x_ref[...] + mod_ref[2:3, :] * (_rms(mix) * g_ref[1:2, :])
    xo_ref[...] = x_new
    f = _rms(x_new) * g_ref[2:3, :] * (1.0 + mod_ref[4:5, :]) + mod_ref[3:4, :]
    f_ref[...] = f
    logits = jnp.dot(f, wr_ref[...], preferred_element_type=F32, precision=lax.Precision.HIGHEST) + br_ref[...]
    lane = lax.broadcasted_iota(jnp.int32, logits.shape, 1).astype(F32)
    logits = jnp.where(lane < n_exp, logits, -jnp.inf)
    vals, idxs = [], []
    for _ in range(TOP_K):
        m = jnp.max(logits, axis=-1, keepdims=True)
        idx = jnp.min(jnp.where(logits == m, lane, float(LANES)), axis=-1, keepdims=True)
        vals.append(m)
        idxs.append(idx)
        logits = jnp.where(lane == idx, -jnp.inf, logits)
    es = [jnp.exp(v - vals[0]) for v in vals]
    den = es[0]
    for e in es[1:]:
        den = den + e
    ti = jnp.zeros_like(lane)
    tg = jnp.zeros_like(lane)
    for k in range(TOP_K):
        ti = jnp.where(lane == k, idxs[k], ti)
        tg = jnp.where(lane == k, es[k] / den, tg)
    ti_ref[...] = ti.astype(jnp.int32)
    tg_ref[...] = tg


def _outproj(lat, ctx, lg_all, x_all, modl, g, wo_bf, wglu_bf, bglu, wr_pad, br_pad, bsz, seq, n_exp):
    tm = ROW_TILE
    d = x_all.shape[1]
    n_lat_tiles = bsz * seq // tm
    has_ctx = ctx is not None
    n_tiles = x_all.shape[0] // tm if has_ctx else n_lat_tiles
    mod_row, _ = _row_maps(bsz, seq, tm)
    row = lambda i: (i, 0)
    lat_row = lambda i: (jnp.minimum(i, n_lat_tiles - 1), 0)
    ctx_row = lambda i: (jnp.maximum(i - n_lat_tiles, 0), 0)
    const = lambda i: (0, 0)
    w = lambda a: pl.BlockSpec(a.shape, const)
    seq_specs = [pl.BlockSpec((tm, a.shape[1]), lat_row) for a in lat]
    seq_specs.insert(2, pl.BlockSpec((tm, lg_all.shape[1]), row))
    args = [lat[0], lat[1], lg_all, lat[2], lat[3]]
    if has_ctx:
        seq_specs += [pl.BlockSpec((tm, a.shape[1]), ctx_row) for a in ctx]
        args += list(ctx)
    r_out = n_tiles * tm
    return pl.pallas_call(
        functools.partial(_outproj_kernel, n_lat_tiles, has_ctx, n_exp),
        grid=(n_tiles,),
        in_specs=seq_specs + [
            pl.BlockSpec((tm, d), row),
            pl.BlockSpec((None, N_MOD, d), lambda i: (mod_row(i), 0, 0)),
            w(g), w(wo_bf), w(wglu_bf), w(bglu), w(wr_pad), w(br_pad),
        ],
        out_specs=[pl.BlockSpec((tm, d), row), pl.BlockSpec((tm, d), row),
                   pl.BlockSpec((tm, LANES), row), pl.BlockSpec((tm, LANES), row)],
        out_shape=[jax.ShapeDtypeStruct((r_out, d), F32), jax.ShapeDtypeStruct((r_out, d), F32),
                   jax.ShapeDtypeStruct((r_out, LANES), jnp.int32), jax.ShapeDtypeStruct((r_out, LANES), F32)],
        compiler_params=_cparams(("arbitrary",)),
        name="outproj_router",
    )(*args, x_all, modl, g, wo_bf, wglu_bf, bglu, wr_pad, br_pad)


def _lru_kernel(reverse, nt, x_ref, prev_ref, next_ref, cw_ref, cb_ref, w_ref, b_ref, sp_ref, h0_ref,
                h_ref, a_s, b_s, carry):
    t = pl.program_id(1)
    tt = nt - 1 - t if reverse else t
    tile = x_ref.shape[0]
    d = x_ref.shape[1]

    @pl.when(t == 0)
    def _():
        carry[...] = jnp.broadcast_to(h0_ref[...], carry.shape)

    x = x_ref[...]
    prev = jnp.where(tt == 0, 0.0, prev_ref[...])
    nxt = jnp.where(tt == nt - 1, 0.0, next_ref[...])
    ext = jnp.concatenate([prev, x, nxt], axis=0)
    c = cb_ref[...]
    for k in range(CONV_W):
        o = SUBLANES + k - CONV_LEFT
        c = c + ext[o:o + tile] * cw_ref[k:k + 1, :]
    z = jnp.dot(c.astype(BF16), w_ref[...], preferred_element_type=F32) + b_ref[...]
    r = jax.nn.sigmoid(z[:, :d])
    i = jax.nn.sigmoid(z[:, d:])
    log_a = -LRU_C * r * sp_ref[...]
    a = jnp.exp(log_a)
    b = jnp.sqrt(1.0 - a * a) * (i * c)

    row = lax.broadcasted_iota(jnp.int32, a.shape, 0) % SUBLANES
    for s in (1, 2, 4):
        if reverse:
            keep = row < SUBLANES - s
            sh = tile - s
        else:
            keep = row >= s
            sh = s
        a_sh = pltpu.roll(a, sh, 0)
        b_sh = pltpu.roll(b, sh, 0)
        b = jnp.where(keep, b + a * b_sh, b)
        a = jnp.where(keep, a * a_sh, a)
    a_s[...] = a
    b_s[...] = b
    ng = tile // SUBLANES

    def body(j, _):
        g = ng - 1 - j if reverse else j
        o = pl.multiple_of(g * SUBLANES, SUBLANES)
        h = b_s[pl.ds(o, SUBLANES), :] + a_s[pl.ds(o, SUBLANES), :] * carry[...]
        h_ref[pl.ds(o, SUBLANES), :] = h
        last = h[0:1, :] if reverse else h[SUBLANES - 1:SUBLANES, :]
        carry[...] = jnp.broadcast_to(last, carry.shape)
        return 0

    lax.fori_loop(0, ng, body, 0)


def _lru_scan(u, row0, bsz, seq, conv_w, conv_b, wcat, bcat, sp, h0, reverse, tile):
    d = u.shape[1]
    nt = seq // tile
    hb = tile // SUBLANES
    t0 = row0 // tile
    assert row0 % tile == 0

    def cur(b, t):
        return (t0 + b * nt + (nt - 1 - t if reverse else t), 0)

    def prv(b, t):
        tt = nt - 1 - t if reverse else t
        return ((t0 + b * nt) * hb + jnp.maximum(tt * hb - 1, 0), 0)

    def nxt(b, t):
        tt = nt - 1 - t if reverse else t
        return ((t0 + b * nt) * hb + jnp.minimum((tt + 1) * hb, nt * hb - 1), 0)

    def out(b, t):
        return (b * nt + (nt - 1 - t if reverse else t), 0)

    const = lambda b, t: (0, 0)
    return pl.pallas_call(
        functools.partial(_lru_kernel, reverse, nt),
        grid=(bsz, nt),
        in_specs=[
            pl.BlockSpec((tile, d), cur),
            pl.BlockSpec((SUBLANES, d), prv),
            pl.BlockSpec((SUBLANES, d), nxt),
            pl.BlockSpec((CONV_W, d), const),
            pl.BlockSpec((1, d), const),
            pl.BlockSpec((d, 2 * d), const),
            pl.BlockSpec((1, 2 * d), const),
            pl.BlockSpec((1, d), const),
            pl.BlockSpec((None, 1, d), lambda b, t: (b, 0, 0)),
        ],
        out_specs=pl.BlockSpec((tile, d), out),
        out_shape=jax.ShapeDtypeStruct((bsz * seq, d), F32),
        scratch_shapes=[pltpu.VMEM((tile, d), F32), pltpu.VMEM((tile, d), F32), pltpu.VMEM((SUBLANES, d), F32)],
        compiler_params=_cparams(("arbitrary", "arbitrary")),
        name="lru_rev" if reverse else "lru_fwd",
    )(u, u, u, conv_w, conv_b, wcat, bcat, sp, h0)


def _block_diag(w):
    n, c, _ = w.shape
    eye = jnp.eye(n, dtype=w.dtype)
    return (eye[:, None, :, None] * w[:, :, None, :]).reshape(n * c, n * c)


LRU_TILE = 512


def _lru_mixer(u_all, bsz, seq, n_ctx, conv_w, conv_b, w_a, b_a, w_x, b_x, lam):
    d = u_all.shape[1]
    hs_l, hs_c = [], []
    for dr, rev in enumerate((False, True)):
        wcat = jnp.concatenate([_block_diag(w_a[dr]), _block_diag(w_x[dr])], axis=1).astype(BF16)
        bcat = jnp.concatenate([b_a[dr], b_x[dr]])[None, :]
        sp = jax.nn.softplus(-lam[dr])[None, :]
        args = (conv_w, conv_b[None, :], wcat, bcat, sp)
        h_c = _lru_scan(u_all, bsz * seq, bsz, n_ctx, *args, jnp.zeros((bsz, 1, d), F32), rev, n_ctx)
        h_c3 = h_c.reshape(bsz, n_ctx, d)
        h_end = h_c3[:, 0:1] if rev else h_c3[:, -1:]
        hs_l.append(_lru_scan(u_all, 0, bsz, seq, *args, h_end, rev, LRU_TILE))
        hs_c.append(h_c)
    return hs_l, hs_c


S5_T = 16
S5_CHUNK_TILE = 256


def _s5_kernel(reverse, add_skip, u_ref, k1_ref, k2r_ref, k2i_ref, k3r_ref, k3i_ref, ar_ref, ai_ref, dsk_ref, h0_ref,
               y_ref, hfin_ref, s_re, s_im, hp_re, hp_im, car_re, car_im):
    t = pl.program_id(1)
    n_groups, ct, _ = u_ref.shape

    @pl.when(t == 0)
    def _():
        car_re[...] = h0_ref[0:1, :]
        car_im[...] = h0_ref[1:2, :]

    for gp in range(n_groups // 2):
        acc_r = jnp.zeros((ct, LANES), F32)
        acc_i = jnp.zeros((ct, LANES), F32)
        for g in (2 * gp, 2 * gp + 1):
            ub = u_ref[g].astype(BF16)
            acc_r = acc_r + jnp.dot(ub, k2r_ref[g], preferred_element_type=F32)
            acc_i = acc_i + jnp.dot(ub, k2i_ref[g], preferred_element_type=F32)
        s_re[:, gp * LANES:(gp + 1) * LANES] = acc_r
        s_im[:, gp * LANES:(gp + 1) * LANES] = acc_i

    ar = ar_ref[...]
    ai = ai_ref[...]

    def body(i, carry):
        hr, hi = carry
        c = ct - 1 - i if reverse else i
        hp_re[pl.ds(c, 1), :] = hr
        hp_im[pl.ds(c, 1), :] = hi
        nr = ar * hr - ai * hi + s_re[pl.ds(c, 1), :]
        ni = ar * hi + ai * hr + s_im[pl.ds(c, 1), :]
        return nr, ni

    hr, hi = lax.fori_loop(0, ct, body, (car_re[...], car_im[...]))
    car_re[...] = hr
    car_im[...] = hi
    hfin_ref[0:1, :] = hr
    hfin_ref[1:2, :] = hi

    for g in range(n_groups):
        gp = g // 2
        u = u_ref[g]
        y = jnp.dot(u.astype(BF16), k1_ref[g], preferred_element_type=F32)
        y = y + jnp.dot(hp_re[:, gp * LANES:(gp + 1) * LANES].astype(BF16), k3r_ref[g], preferred_element_type=F32)
        y = y + jnp.dot(hp_im[:, gp * LANES:(gp + 1) * LANES].astype(BF16), k3i_ref[g], preferred_element_type=F32)
        if add_skip:
            y = y + u * dsk_ref[g]
        y_ref[g] = y


def _s5_scan(ug, tabs, h0, reverse, add_skip):
    bsz, n_groups, nc, w = ug.shape
    ct = min(S5_CHUNK_TILE, nc)
    nt = nc // ct
    k1, k2r, k2i, k3r, k3i, ar, ai, dsk = tabs
    ns = ar.shape[1]
    cur = lambda b, t: (b, 0, nt - 1 - t if reverse else t, 0)
    c3 = lambda b, t: (0, 0, 0)
    c2 = lambda b, t: (0, 0)
    st = lambda b, t: (b, 0, 0)
    return pl.pallas_call(
        functools.partial(_s5_kernel, reverse, add_skip),
        grid=(bsz, nt),
        in_specs=[
            pl.BlockSpec((None, n_groups, ct, w), cur),
            pl.BlockSpec(k1.shape, c3), pl.BlockSpec(k2r.shape, c3), pl.BlockSpec(k2i.shape, c3),
            pl.BlockSpec(k3r.shape, c3), pl.BlockSpec(k3i.shape, c3),
            pl.BlockSpec(ar.shape, c2), pl.BlockSpec(ai.shape, c2), pl.BlockSpec(dsk.shape, c3),
            pl.BlockSpec((None, 2, ns), st),
        ],
        out_specs=[pl.BlockSpec((None, n_groups, ct, w), cur), pl.BlockSpec((None, 2, ns), st)],
        out_shape=[jax.ShapeDtypeStruct(ug.shape, F32), jax.ShapeDtypeStruct((bsz, 2, ns), F32)],
        scratch_shapes=[pltpu.VMEM((ct, ns), F32)] * 4 + [pltpu.VMEM((1, ns), F32)] * 2,
        compiler_params=_cparams(("arbitrary", "arbitrary")),
        name="s5_rev" if reverse else "s5_fwd",
    )(ug, k1, k2r, k2i, k3r, k3i, ar, ai, dsk, h0)


def _s5_tables(a_re, a_im, log_dt, b_re, b_im, c_re, c_im, d_skip, reverse):
    n_groups, n_state = a_re.shape
    n_ch = b_re.shape[2]
    tt = S5_T
    dt = jnp.exp(log_dt)[:, None]
    lre, lim = a_re * dt, a_im * dt
    mag = jnp.exp(lre)
    abar_r, abar_i = mag * jnp.cos(lim), mag * jnp.sin(lim)
    den = a_re * a_re + a_im * a_im
    nr = abar_r - 1.0
    coef_r = (nr * a_re + abar_i * a_im) / den
    coef_i = (abar_i * a_re - nr * a_im) / den
    bb_r = coef_r[..., None] * b_re - coef_i[..., None] * b_im
    bb_i = coef_r[..., None] * b_im + coef_i[..., None] * b_re
    dl = jnp.arange(tt + 1, dtype=F32)[:, None, None]
    pw_r = jnp.exp(dl * lre) * jnp.cos(dl * lim)
    pw_i = jnp.exp(dl * lre) * jnp.sin(dl * lim)
    cp_r = c_re[None] * pw_r[:, :, None, :] - c_im[None] * pw_i[:, :, None, :]
    cp_i = c_re[None] * pw_i[:, :, None, :] + c_im[None] * pw_r[:, :, None, :]
    m = jnp.einsum('dghp,gpk->dghk', cp_r, bb_r) - jnp.einsum('dghp,gpk->dghk', cp_i, bb_i)
    j = jnp.arange(tt)
    lag = (j[:, None] - j[None, :]) if reverse else (j[None, :] - j[:, None])
    k1 = jnp.where((lag >= 0)[:, :, None, None, None], m[jnp.clip(lag, 0, tt)], 0.0)
    k1 = k1.transpose(2, 0, 4, 1, 3).reshape(n_groups, tt * n_ch, tt * n_ch)
    e_in = j if reverse else tt - 1 - j
    k2_r = pw_r[e_in][..., None] * bb_r[None] - pw_i[e_in][..., None] * bb_i[None]
    k2_i = pw_r[e_in][..., None] * bb_i[None] + pw_i[e_in][..., None] * bb_r[None]
    e_out = tt - j if reverse else j + 1
    k3_r = cp_r[e_out].transpose(1, 3, 0, 2).reshape(n_groups, n_state, tt * n_ch)
    k3_i = -cp_i[e_out].transpose(1, 3, 0, 2).reshape(n_groups, n_state, tt * n_ch)

    def pad_cols(k):
        k = k.transpose(1, 0, 3, 2).reshape(n_groups, tt * n_ch, n_state)
        z = jnp.zeros_like(k)
        odd = (jnp.arange(n_groups) % 2 == 1)[:, None, None]
        return jnp.where(odd, jnp.concatenate([z, k], axis=2), jnp.concatenate([k, z], axis=2))

    def pad_rows(k):
        z = jnp.zeros_like(k)
        odd = (jnp.arange(n_groups) % 2 == 1)[:, None, None]
        return jnp.where(odd, jnp.concatenate([z, k], axis=1), jnp.concatenate([k, z], axis=1))

    dsk = jnp.tile(d_skip.reshape(n_groups, 1, n_ch), (1, 1, tt))
    return (k1.astype(BF16), pad_cols(k2_r).astype(BF16), pad_cols(k2_i).astype(BF16),
            pad_rows(k3_r).astype(BF16), pad_rows(k3_i).astype(BF16),
            pw_r[tt].reshape(1, -1), pw_i[tt].reshape(1, -1), dsk)


def _s5_mixer(u_all, bsz, seq, n_ctx, a_re, a_im, log_dt, b_re, b_im, c_re, c_im, d_skip):
    n_groups, n_state = a_re.shape[1], a_re.shape[2]
    n_ch = b_re.shape[3]

    def to_groups(u):
        nc = u.shape[0] // (bsz * S5_T)
        return u.reshape(bsz, nc, S5_T, n_groups, n_ch).transpose(0, 3, 1, 2, 4).reshape(bsz, n_groups, nc, S5_T * n_ch)

    def from_groups(y):
        nc = y.shape[2]
        return y.reshape(bsz, n_groups, nc, S5_T, n_ch).transpose(0, 2, 3, 1, 4).reshape(bsz * nc * S5_T, n_groups * n_ch)

    ul, uc = to_groups(u_all[:bsz * seq]), to_groups(u_all[bsz * seq:])
    zero = jnp.zeros((bsz, 2, n_groups * n_state), F32)
    y_l, y_c = None, None
    for dr, rev in enumerate((False, True)):
        tabs = _s5_tables(a_re[dr], a_im[dr], log_dt[dr], b_re[dr], b_im[dr], c_re[dr], c_im[dr], d_skip, rev)
        yc, h_end = _s5_scan(uc, tabs, zero, rev, dr == 0)
        yl, _ = _s5_scan(ul, tabs, h_end, rev, dr == 0)
        y_l = yl if y_l is None else y_l + yl
        y_c = yc if y_c is None else y_c + yc
    return from_groups(y_l), from_groups(y_c)


NA_RB = 8
NA_HALO = NA_KH // 2
_NT = (((1,), (1,)), ((), ()))


def _pair_softmax_out(h, lane, out, s_list, v_list):
    mx = None
    for s in s_list:
        m = jnp.max(s, axis=-1, keepdims=True)
        mx = m if mx is None else jnp.maximum(mx, m)
    den, o = None, None
    for s, v in zip(s_list, v_list):
        p = jnp.exp(s - mx)
        d = jnp.sum(p, axis=-1, keepdims=True)
        den = d if den is None else den + d
        pv = jnp.dot(p.astype(BF16), v, preferred_element_type=F32)
        o = pv if o is None else o + pv
    o = o / den
    return o if out is None else jnp.where(lane // NA_HD == h, o, out)


def _na_kernel(scale, q_ref, qr_ref, kp_ref, kc_ref, kn_ref, vp_ref, vc_ref, vn_ref, kx_ref, vx_ref, bias_ref, o_ref):
    lane = lax.broadcasted_iota(jnp.int32, (1, LANES), 1)
    kcat = jnp.concatenate([kp_ref[...], kc_ref[...], kn_ref[...]], axis=0)
    vcat = jnp.concatenate([vp_ref[...], vc_ref[...], vn_ref[...]], axis=0)
    q, qr, kx, vx = q_ref[...], qr_ref[...], kx_ref[...], vx_ref[...]
    out = None
    for h in range(LANES // NA_HD):
        sel = lane // NA_HD == h
        s_loc = lax.dot_general(jnp.where(sel, qr, 0).astype(BF16), kcat, _NT, preferred_element_type=F32)
        s_loc = s_loc * scale + bias_ref[h].astype(F32)
        s_ctx = lax.dot_general(jnp.where(sel, q, 0).astype(BF16), kx, _NT, preferred_element_type=F32) * scale
        out = _pair_softmax_out(h, lane, out, [s_loc, s_ctx], [vcat, vx])
    o_ref[...] = out.astype(o_ref.dtype)


def _na_bias(rpb, rows):
    n_kk = NA_RB + 2 * NA_HALO
    i = jnp.arange(NA_RB)
    kk = jnp.arange(n_kk)
    c = jnp.arange(GRID_W)
    ws = jnp.clip(c - NA_KW // 2, 0, GRID_W - NA_KW)
    col_ok = (c[None, :] >= ws[:, None]) & (c[None, :] < ws[:, None] + NA_KW)
    dc = jnp.clip(c[None, :] - c[:, None] + NA_KW - 1, 0, 2 * NA_KW - 2)
    tabs = []
    for r0 in (0, NA_RB, rows - NA_RB):
        r = r0 + i
        rs = jnp.clip(r - NA_KH // 2, 0, rows - NA_KH)
        kr = r0 - NA_HALO + kk
        row_ok = (kr[None, :] >= rs[:, None]) & (kr[None, :] < rs[:, None] + NA_KH)
        dr = jnp.clip(kr[None, :] - r[:, None] + NA_KH - 1, 0, 2 * NA_KH - 2)
        b = rpb[:, dr][:, :, :, dc]
        ok = row_ok[:, :, None, None] & col_ok[None, None]
        b = jnp.where(ok[None], b, NEG_BIG).transpose(0, 1, 3, 2, 4)
        tabs.append(b.reshape(rpb.shape[0], NA_RB * GRID_W, n_kk * GRID_W))
    return jnp.stack(tabs).astype(BF16)


def _na_lat(q, qr, k, v, bsz, seq, n_ctx, rpb):
    d_att = q.shape[1]
    rows = seq // GRID_W
    nrb = rows // NA_RB
    assert nrb >= 2 and NA_RB == 2 * NA_HALO and (bsz * seq) % n_ctx == 0
    tq = NA_RB * GRID_W
    th = NA_HALO * GRID_W
    c0 = bsz * seq // n_ctx
    bias = _na_bias(rpb, rows)
    heads_per_pair = LANES // NA_HD
    cur = lambda p, b, r: (b * nrb + r, p)
    prv = lambda p, b, r: (b * 2 * nrb + jnp.maximum(2 * r - 1, 0), p)
    nxt = lambda p, b, r: (b * 2 * nrb + jnp.minimum(2 * r + 2, 2 * nrb - 1), p)
    cx = lambda p, b, r: (c0 + b, p)
    var = lambda p, b, r: (jnp.where(r == 0, 0, jnp.where(r == nrb - 1, 2, 1)), p, 0, 0)
    bs = lambda n, f: pl.BlockSpec((n, LANES), f)
    return pl.pallas_call(
        functools.partial(_na_kernel, NA_HD ** -0.5),
        grid=(d_att // LANES, bsz, nrb),
        in_specs=[bs(tq, cur), bs(tq, cur), bs(th, prv), bs(tq, cur), bs(th, nxt), bs(th, prv), bs(tq, cur), bs(th, nxt),
                  bs(n_ctx, cx), bs(n_ctx, cx),
                  pl.BlockSpec((None, heads_per_pair, tq, tq + 2 * th), var)],
        out_specs=bs(tq, cur),
        out_shape=jax.ShapeDtypeStruct((bsz * seq, d_att), BF16),
        compiler_params=_cparams(("arbitrary", "arbitrary", "arbitrary")),
        name="na_lat",
    )(q, qr, k, k, k, v, v, v, k, v, bias)


def _na_ctx_kernel(scale, q_ref, k_ref, v_ref, o_ref):
    lane = lax.broadcasted_iota(jnp.int32, (1, LANES), 1)
    q, k, v = q_ref[...], k_ref[...], v_ref[...]
    out = None
    for h in range(LANES // NA_HD):
        s = lax.dot_general(jnp.where(lane // NA_HD == h, q, 0).astype(BF16), k, _NT, preferred_element_type=F32) * scale
        out = _pair_softmax_out(h, lane, out, [s], [v])
    o_ref[...] = out.astype(o_ref.dtype)


def _na_ctx(q, k, v, bsz, seq, n_ctx):
    d_att = q.shape[1]
    c0 = bsz * seq // n_ctx
    blk = pl.BlockSpec((n_ctx, LANES), lambda p, b: (c0 + b, p))
    return pl.pallas_call(
        functools.partial(_na_ctx_kernel, NA_HD ** -0.5),
        grid=(d_att // LANES, bsz),
        in_specs=[blk, blk, blk],
        out_specs=pl.BlockSpec((n_ctx, LANES), lambda p, b: (b, p)),
        out_shape=jax.ShapeDtypeStruct((bsz * n_ctx, d_att), BF16),
        compiler_params=_cparams(("arbitrary", "arbitrary")),
        name="na_ctx",
    )(q, k, v)


MOE_BLOCK = 256
COMBINE_TILE = 256
GATHER_UNROLL = 8


def _start_row_gathers(src_hbm, idx_ref, n_rows, dst_ref, sem):
    def body(c, _):
        for u in range(GATHER_UNROLL):
            j = c * GATHER_UNROLL + u
            pltpu.make_async_copy(src_hbm.at[pl.ds(idx_ref[0, j], 1), :], dst_ref.at[pl.ds(j, 1), :], sem).start()
        return 0

    lax.fori_loop(0, n_rows // GATHER_UNROLL, body, 0)


def _wait_row_gathers(src_hbm, n_rows, dst_ref, sem):
    pltpu.make_async_copy(src_hbm.at[pl.ds(0, n_rows), :], dst_ref, sem).wait()


def _expert_kernel(d_ff, be_ref, tok_ref, tokn_ref, f_hbm, wgu_ref, bgu_ref, wdn_ref, bdn_ref, y_ref,
                   xbuf, sems, wgu_bf, wdn_bf):
    i = pl.program_id(0)
    n = pl.num_programs(0)
    slot = i % 2
    rows = xbuf.shape[1]

    @pl.when(i == 0)
    def _():
        _start_row_gathers(f_hbm, tok_ref, rows, xbuf.at[0], sems.at[0])

    _wait_row_gathers(f_hbm, rows, xbuf.at[slot], sems.at[slot])

    @pl.when(i + 1 < n)
    def _():
        _start_row_gathers(f_hbm, tokn_ref, rows, xbuf.at[1 - slot], sems.at[1 - slot])

    @pl.when((i == 0) | (be_ref[i] != be_ref[jnp.maximum(i - 1, 0)]))
    def _():
        wgu_bf[...] = wgu_ref[...].astype(BF16)
        wdn_bf[...] = wdn_ref[...].astype(BF16)

    x = xbuf[slot].astype(BF16)
    gu = jnp.dot(x, wgu_bf[...], preferred_element_type=F32) + bgu_ref[...]
    glu = jnp.minimum(gu[:, :d_ff], SWIGLU_LIMIT)
    lin = jnp.clip(gu[:, d_ff:], -SWIGLU_LIMIT, SWIGLU_LIMIT)
    act = glu * jax.nn.sigmoid(SWIGLU_ALPHA * glu) * (lin + 1.0)
    y_ref[...] = jnp.dot(act.astype(BF16), wdn_bf[...], preferred_element_type=F32) + bdn_ref[...]


def _experts(f_all, blk_e, tok_buf, w_gu, b_gu, w_down, b_down):
    n_exp, d, d_gu = w_gu.shape
    d_ff = w_down.shape[1]
    nb = blk_e.shape[0]
    tokspec = lambda f: pl.BlockSpec((None, 1, MOE_BLOCK), f, memory_space=pltpu.SMEM)
    wspec = lambda shp: pl.BlockSpec((None,) + shp, lambda i, be: (be[i], 0, 0))
    return pl.pallas_call(
        functools.partial(_expert_kernel, d_ff),
        grid_spec=pltpu.PrefetchScalarGridSpec(
            num_scalar_prefetch=1,
            grid=(nb,),
            in_specs=[
                tokspec(lambda i, be: (i, 0, 0)),
                tokspec(lambda i, be: (jnp.minimum(i + 1, nb - 1), 0, 0)),
                pl.BlockSpec(memory_space=pl.ANY),
                wspec((d, d_gu)), wspec((1, d_gu)), wspec((d_ff, d)), wspec((1, d)),
            ],
            out_specs=pl.BlockSpec((MOE_BLOCK, d), lambda i, be: (i, 0)),
            scratch_shapes=[pltpu.VMEM((2, MOE_BLOCK, d), F32), pltpu.SemaphoreType.DMA((2,)),
                            pltpu.VMEM((d, d_gu), BF16), pltpu.VMEM((d_ff, d), BF16)],
        ),
        out_shape=jax.ShapeDtypeStruct((nb * MOE_BLOCK, d), F32),
        compiler_params=_cparams(("arbitrary",)),
        name="experts",
    )(blk_e, tok_buf, tok_buf, f_all, w_gu, b_gu.reshape(n_exp, 1, d_gu), w_down, b_down.reshape(n_exp, 1, d))


def _combine_kernel(dst_ref, dstn_ref, y_hbm, tg_ref, x_ref, mod_ref, g_ref, o_ref, ybuf, sems):
    i = pl.program_id(0)
    n = pl.num_programs(0)
    slot = i % 2
    tm = x_ref.shape[0]
    n_rows = TOP_K * tm

    @pl.when(i == 0)
    def _():
        _start_row_gathers(y_hbm, dst_ref, n_rows, ybuf.at[0], sems.at[0])

    _wait_row_gathers(y_hbm, n_rows, ybuf.at[slot], sems.at[slot])

    @pl.when(i + 1 < n)
    def _():
        _start_row_gathers(y_hbm, dstn_ref, n_rows, ybuf.at[1 - slot], sems.at[1 - slot])

    tg = tg_ref[...]
    y = None
    for k in range(TOP_K):
        yk = ybuf[slot, k * tm:(k + 1) * tm, :] * tg[:, k:k + 1]
        y = yk if y is None else y + yk
    o_ref[...] = x_ref[...] + mod_ref[5:6, :] * (_rms(y) * g_ref[3:4, :])


def _combine(y_buf, dest, tg, x_new, modl, g, bsz, seq):
    tm = COMBINE_TILE
    d = x_new.shape[1]
    nt = dest.shape[0]
    mod_row, _ = _row_maps(bsz, seq, tm)
    dspec = lambda f: pl.BlockSpec((None, 1, TOP_K * tm), f, memory_space=pltpu.SMEM)
    row = lambda i: (i, 0)
    return pl.pallas_call(
        _combine_kernel,
        grid=(nt,),
        in_specs=[
            dspec(lambda i: (i, 0, 0)),
            dspec(lambda i: (jnp.minimum(i + 1, nt - 1), 0, 0)),
            pl.BlockSpec(memory_space=pl.ANY),
            pl.BlockSpec((tm, LANES), row),
            pl.BlockSpec((tm, d), row),
            pl.BlockSpec((None, N_MOD, d), lambda i: (mod_row(i), 0, 0)),
            pl.BlockSpec(g.shape, lambda i: (0, 0)),
        ],
        out_specs=pl.BlockSpec((tm, d), row),
        out_shape=jax.ShapeDtypeStruct((nt * tm, d), F32),
        scratch_shapes=[pltpu.VMEM((2, TOP_K * tm, d), F32), pltpu.SemaphoreType.DMA((2,))],
        compiler_params=_cparams(("arbitrary",)),
        name="combine",
    )(dest, dest, y_buf, tg, x_new, modl, g)


def _routing_tables(ti, n_tok, n_exp):
    e = ti[:n_tok, :TOP_K]
    e_flat = e.reshape(-1)
    n_assign = n_tok * TOP_K
    onehot = (e[:, :, None] == jnp.arange(n_exp, dtype=jnp.int32)).sum(axis=1).astype(jnp.int32)
    csum = jnp.cumsum(onehot, axis=0)
    counts = csum[-1]
    rank = jnp.take_along_axis(csum - onehot, e, axis=1)
    padded = (counts + MOE_BLOCK - 1) // MOE_BLOCK * MOE_BLOCK
    start = jnp.cumsum(counts) - counts
    p_end = jnp.cumsum(padded)
    p_start = p_end - padded
    nb = -(-n_assign // MOE_BLOCK) + n_exp
    blk_e = jnp.minimum(jnp.searchsorted(p_end, jnp.arange(nb, dtype=jnp.int32) * MOE_BLOCK, side='right'),
                        n_exp - 1).astype(jnp.int32)
    dest = p_start[e] + rank
    order = jnp.argsort(e_flat)
    slot = jnp.arange(nb * MOE_BLOCK, dtype=jnp.int32)
    e_s = blk_e[slot // MOE_BLOCK]
    off = slot - p_start[e_s]
    valid = off < counts[e_s]
    src = jnp.clip(start[e_s] + off, 0, n_assign - 1)
    tok_buf = jnp.where(valid, order[src] // TOP_K, 0).astype(jnp.int32)
    nt = n_tok // COMBINE_TILE
    dest_tiles = dest.reshape(nt, COMBINE_TILE, TOP_K).transpose(0, 2, 1).reshape(nt, 1, TOP_K * COMBINE_TILE)
    return blk_e, tok_buf.reshape(nb, 1, MOE_BLOCK), dest_tiles.astype(jnp.int32)


def _moe(f_all, ti, tg, x_new, modl, g, w_gu, b_gu, w_down, b_down, n_tok, bsz, seq):
    n_exp = w_gu.shape[0]
    blk_e, tok_buf, dest = _routing_tables(ti, n_tok, n_exp)
    y_buf = _experts(f_all, blk_e, tok_buf, w_gu, b_gu, w_down, b_down)
    return _combine(y_buf, dest, tg, x_new, modl, g, bsz, seq)


def kernel(x, c, ctx, c_ctx, w_mod, b_mod, norm_g, w_in, w_out, lru_conv_w, lru_conv_b, lru_w_a, lru_b_a, lru_w_x,
           lru_b_x, lru_lambda, s5_a_re, s5_a_im, s5_log_dt, s5_b_re, s5_b_im, s5_c_re, s5_c_im, s5_d, s5_w_glu,
           s5_b_glu, na_rpb, moe_w_router, moe_b_router, moe_w_gu, moe_b_gu, moe_w_down, moe_b_down):
    bsz, seq, d = x.shape
    n_ctx = ctx.shape[1]
    depth = w_mod.shape[0]
    d_lru = lru_conv_w.shape[2]
    d_s5 = s5_d.shape[1]
    d_att = w_out.shape[1] - d_lru - d_s5
    n_exp = moe_w_router.shape[2]
    n_lat = bsz * seq
    assert bsz * n_ctx == ROW_TILE and bsz + 1 <= SUBLANES and n_exp <= LANES

    cvec = jnp.zeros((SUBLANES, d), F32).at[:bsz].set(c).at[bsz].set(c_ctx)
    mods = _modulation(cvec, w_mod, b_mod).reshape(depth, SUBLANES, N_MOD, d)
    cos_t, sin_t = _rope_tables(seq)
    x_all = jnp.concatenate([x.reshape(n_lat, d), ctx.reshape(bsz * n_ctx, d)], axis=0)

    for l in range(depth):
        need_ctx = l < depth - 1
        modl, g = mods[l], norm_g[l]
        lu, lg, su, q, qr, kr, v = _inproj(x_all, modl, g[0:1], w_in[l].astype(BF16), cos_t, sin_t,
                                           (d_lru, d_s5, d_att), bsz, seq)
        hs_l, hs_c = _lru_mixer(lu, bsz, seq, n_ctx, lru_conv_w[l], lru_conv_b[l], lru_w_a[l], lru_b_a[l],
                                lru_w_x[l], lru_b_x[l], lru_lambda[l])
        ys_l, ys_c = _s5_mixer(su, bsz, seq, n_ctx, s5_a_re[l], s5_a_im[l], s5_log_dt[l], s5_b_re[l], s5_b_im[l],
                               s5_c_re[l], s5_c_im[l], s5_d[l])
        na_l = _na_lat(q, qr, kr, v, bsz, seq, n_ctx, na_rpb[l])
        lat = (hs_l[0], hs_l[1], ys_l, na_l)
        cx = (hs_c[0], hs_c[1], ys_c, _na_ctx(q, kr, v, bsz, seq, n_ctx)) if need_ctx else None
        wr_pad = jnp.zeros((d, LANES), F32).at[:, :n_exp].set(moe_w_router[l])
        br_pad = jnp.zeros((1, LANES), F32).at[0, :n_exp].set(moe_b_router[l])
        x_new, f_all, ti, tg = _outproj(lat, cx, lg, x_all, modl, g, w_out[l].astype(BF16), s5_w_glu[l].astype(BF16),
                                        s5_b_glu[l][None, :], wr_pad, br_pad, bsz, seq, n_exp)
        n_tok = x_new.shape[0]
        x_all = _moe(f_all, ti, tg, x_new, modl, g, moe_w_gu[l], moe_b_gu[l], moe_w_down[l], moe_b_down[l],
                     n_tok, bsz, seq)
    return x_all[:n_lat].reshape(bsz, seq, d)
```

```python
import functools

import jax
import jax.numpy as jnp
from jax import lax
from jax.experimental import pallas as pl
from jax.experimental.pallas import tpu as pltpu

F32 = jnp.float32
BF16 = jnp.bfloat16

EPS = 1e-6
N_MOD = 6
LRU_C = 8.0
CONV_W = 4
CONV_LEFT = 2
GRID_W = 64
NA_HD = 64
NA_KH = 8
NA_KW = 16
ROPE_BASE = 10000.0
TOP_K = 4
SWIGLU_LIMIT = 7.0
SWIGLU_ALPHA = 1.702

LANES = 128
SUBLANES = 8
VMEM_LIMIT_BYTES = 56 * 1024 * 1024

NEG_BIG = -1e30


def _cparams(sem):
    return pltpu.CompilerParams(dimension_semantics=sem, vmem_limit_bytes=VMEM_LIMIT_BYTES)


def _rms(x):
    return x * lax.rsqrt(jnp.mean(x * x, axis=-1, keepdims=True) + EPS)


def _mod_kernel(c_ref, w_ref, b_ref, o_ref):
    c = c_ref[...]
    s = c * jax.nn.sigmoid(c)
    o_ref[...] = jnp.dot(s, w_ref[...], preferred_element_type=F32, precision=lax.Precision.HIGHEST) + b_ref[...]


def _modulation(cvec, w_mod, b_mod):
    n_layers, d, nd = w_mod.shape
    tn = nd // 4
    return pl.pallas_call(
        _mod_kernel,
        grid=(n_layers, nd // tn),
        in_specs=[
            pl.BlockSpec((SUBLANES, d), lambda l, j: (0, 0)),
            pl.BlockSpec((None, d, tn), lambda l, j: (l, 0, j)),
            pl.BlockSpec((None, 1, tn), lambda l, j: (l, 0, j)),
        ],
        out_specs=pl.BlockSpec((None, SUBLANES, tn), lambda l, j: (l, 0, j)),
        out_shape=jax.ShapeDtypeStruct((n_layers, SUBLANES, nd), F32),
        compiler_params=_cparams(("arbitrary", "arbitrary")),
        name="modulation",
    )(cvec, w_mod, b_mod.reshape(n_layers, 1, nd))


def _swap16(x):
    w = x.shape[-1]
    lane = lax.broadcasted_iota(jnp.int32, x.shape, 1)
    return jnp.where(lane % 32 < 16, pltpu.roll(x, w - 16, 1), pltpu.roll(x, 16, 1))


def _inproj_kernel(d_lru, d_s5, d_att, x_ref, mod_ref, g_ref, w_ref, cos_ref, sin_ref,
                   lu_ref, lg_ref, su_ref, q_ref, qr_ref, kr_ref, v_ref):
    h = _rms(x_ref[...]) * g_ref[...]
    h = h * (1.0 + mod_ref[1:2, :]) + mod_ref[0:1, :]
    y = jnp.dot(h.astype(BF16), w_ref[...], preferred_element_type=F32)
    o = 0
    lu_ref[...] = y[:, o:o + d_lru]
    o += d_lru
    lg_ref[...] = y[:, o:o + d_lru]
    o += d_lru
    su_ref[...] = y[:, o:o + d_s5]
    o += d_s5
    q = y[:, o:o + d_att]
    o += d_att
    k = y[:, o:o + d_att]
    o += d_att
    v = y[:, o:o + d_att]
    reps = d_att // LANES
    cos = jnp.concatenate([cos_ref[...]] * reps, axis=1)
    sin = jnp.concatenate([sin_ref[...]] * reps, axis=1)
    q_ref[...] = q.astype(BF16)
    qr_ref[...] = (q * cos + _swap16(q) * sin).astype(BF16)
    kr_ref[...] = (k * cos + _swap16(k) * sin).astype(BF16)
    v_ref[...] = v.astype(BF16)


ROW_TILE = 512


def _row_maps(bsz, seq, tm):
    assert seq % tm == 0
    tps = seq // tm
    n_lat = bsz * tps
    mod_row = lambda i: jnp.where(i < n_lat, i // tps, bsz)
    rope_tile = lambda i: jnp.where(i < n_lat, i % tps, tps)
    return mod_row, rope_tile


def _inproj(x2, modl, g0, w_in_bf, cos_t, sin_t, dims, bsz, seq):
    d_lru, d_s5, d_att = dims
    r, d = x2.shape
    d_in = w_in_bf.shape[1]
    tm = ROW_TILE
    assert r % tm == 0
    mod_row, rope_tile = _row_maps(bsz, seq, tm)
    row = lambda i: (i, 0)
    outs = [jax.ShapeDtypeStruct((r, d_lru), F32)] * 2 + [jax.ShapeDtypeStruct((r, d_s5), F32)] + \
           [jax.ShapeDtypeStruct((r, d_att), BF16)] * 4
    return pl.pallas_call(
        functools.partial(_inproj_kernel, d_lru, d_s5, d_att),
        grid=(r // tm,),
        in_specs=[
            pl.BlockSpec((tm, d), row),
            pl.BlockSpec((None, N_MOD, d), lambda i: (mod_row(i), 0, 0)),
            pl.BlockSpec((1, d), lambda i: (0, 0)),
            pl.BlockSpec((d, d_in), lambda i: (0, 0)),
            pl.BlockSpec((tm, LANES), lambda i: (rope_tile(i), 0)),
            pl.BlockSpec((tm, LANES), lambda i: (rope_tile(i), 0)),
        ],
        out_specs=[pl.BlockSpec((tm, d_lru), row)] * 2 + [pl.BlockSpec((tm, d_s5), row)] +
                  [pl.BlockSpec((tm, d_att), row)] * 4,
        out_shape=outs,
        compiler_params=_cparams(("arbitrary",)),
        name="inproj",
    )(x2, modl, g0, w_in_bf, cos_t, sin_t)


def _rope_tables(seq):
    nf = NA_HD // 4
    inv = ROPE_BASE ** (-jnp.arange(nf, dtype=F32) / nf)
    t = jnp.arange(seq)
    ang_r = (t // GRID_W).astype(F32)[:, None] * inv
    ang_c = (t % GRID_W).astype(F32)[:, None] * inv
    cos = jnp.concatenate([jnp.cos(ang_r)] * 2 + [jnp.cos(ang_c)] * 2, axis=1)
    sin = jnp.concatenate([-jnp.sin(ang_r), jnp.sin(ang_r), -jnp.sin(ang_c), jnp.sin(ang_c)], axis=1)
    cos = jnp.concatenate([cos, jnp.ones((ROW_TILE, NA_HD), F32)], axis=0)
    sin = jnp.concatenate([sin, jnp.zeros((ROW_TILE, NA_HD), F32)], axis=0)
    return jnp.concatenate([cos, cos], axis=1), jnp.concatenate([sin, sin], axis=1)


def _outproj_kernel(n_lat_tiles, has_ctx, n_exp, *refs):
    if has_ctx:
        (h0_ref, h1_ref, lg_ref, ys_ref, na_ref, h0c_ref, h1c_ref, ysc_ref, nac_ref,
         x_ref, mod_ref, g_ref, wo_ref, wglu_ref, bglu_ref, wr_ref, br_ref, xo_ref, f_ref, ti_ref, tg_ref) = refs
        is_ctx = pl.program_id(0) >= n_lat_tiles
        pick = lambda a, c: jnp.where(is_ctx, c[...], a[...])
        h0, h1, ys, na = pick(h0_ref, h0c_ref), pick(h1_ref, h1c_ref), pick(ys_ref, ysc_ref), pick(na_ref, nac_ref)
    else:
        (h0_ref, h1_ref, lg_ref, ys_ref, na_ref,
         x_ref, mod_ref, g_ref, wo_ref, wglu_ref, bglu_ref, wr_ref, br_ref, xo_ref, f_ref, ti_ref, tg_ref) = refs
        h0, h1, ys, na = h0_ref[...], h1_ref[...], ys_ref[...], na_ref[...]
    d_lru, d_s5 = h0.shape[1], ys.shape[1]
    y_lru = (h0 + h1) * jax.nn.gelu(lg_ref[...])
    z = jax.nn.gelu(ys)
    y_s5 = z * jax.nn.sigmoid(jnp.dot(z.astype(BF16), wglu_ref[...], preferred_element_type=F32) + bglu_ref[...])
    mix = jnp.dot(y_lru.astype(BF16), wo_ref[0:d_lru, :], preferred_element_type=F32)
    mix = mix + jnp.dot(y_s5.astype(BF16), wo_ref[d_lru:d_lru + d_s5, :], preferred_element_type=F32)
    mix = mix + jnp.dot(na, wo_ref[d_lru + d_s5:, :], preferred_element_type=F32)
    x_new = x_ref[...] + mod_ref[2:3, :] * (_rms(mix) * g_ref[1:2, :])
    xo_ref[...] = x_new
    f = _rms(x_new) * g_ref[2:3, :] * (1.0 + mod_ref[4:5, :]) + mod_ref[3:4, :]
    f_ref[...] = f
    logits = jnp.dot(f, wr_ref[...], preferred_element_type=F32, precision=lax.Precision.HIGHEST) + br_ref[...]
    lane = lax.broadcasted_iota(jnp.int32, logits.shape, 1).astype(F32)
    logits = jnp.where(lane < n_exp, logits, -jnp.inf)
    vals, idxs = [], []
    for _ in range(TOP_K):
        m = jnp.max(logits, axis=-1, keepdims=True)
        idx = jnp.min(jnp.where(logits == m, lane, float(LANES)), axis=-1, keepdims=True)
        vals.append(m)
        idxs.append(idx)
        logits = jnp.where(lane == idx, -jnp.inf, logits)
    es = [jnp.exp(v - vals[0]) for v in vals]
    den = es[0]
    for e in es[1:]:
        den = den + e
    ti = jnp.zeros_like(lane)
    tg = jnp.zeros_like(lane)
    for k in range(TOP_K):
        ti = jnp.where(lane == k, idxs[k], ti)
        tg = jnp.where(lane == k, es[k] / den, tg)
    ti_ref[...] = ti.astype(jnp.int32)
    tg_ref[...] = tg


def _outproj(lat, ctx, lg_all, x_all, modl, g, wo_bf, wglu_bf, bglu, wr_pad, br_pad, bsz, seq, n_exp):
    tm = ROW_TILE
    d = x_all.shape[1]
    n_lat_tiles = bsz * seq // tm
    has_ctx = ctx is not None
    n_tiles = x_all.shape[0] // tm if has_ctx else n_lat_tiles
    mod_row, _ = _row_maps(bsz, seq, tm)
    row = lambda i: (i, 0)
    lat_row = lambda i: (jnp.minimum(i, n_lat_tiles - 1), 0)
    ctx_row = lambda i: (jnp.maximum(i - n_lat_tiles, 0), 0)
    const = lambda i: (0, 0)
    w = lambda a: pl.BlockSpec(a.shape, const)
    seq_specs = [pl.BlockSpec((tm, a.shape[1]), lat_row) for a in lat]
    seq_specs.insert(2, pl.BlockSpec((tm, lg_all.shape[1]), row))
    args = [lat[0], lat[1], lg_all, lat[2], lat[3]]
    if has_ctx:
        seq_specs += [pl.BlockSpec((tm, a.shape[1]), ctx_row) for a in ctx]
        args += list(ctx)
    r_out = n_tiles * tm
    return pl.pallas_call(
        functools.partial(_outproj_kernel, n_lat_tiles, has_ctx, n_exp),
        grid=(n_tiles,),
        in_specs=seq_specs + [
            pl.BlockSpec((tm, d), row),
            pl.BlockSpec((None, N_MOD, d), lambda i: (mod_row(i), 0, 0)),
            w(g), w(wo_bf), w(wglu_bf), w(bglu), w(wr_pad), w(br_pad),
        ],
        out_specs=[pl.BlockSpec((tm, d), row), pl.BlockSpec((tm, d), row),
                   pl.BlockSpec((tm, LANES), row), pl.BlockSpec((tm, LANES), row)],
        out_shape=[jax.ShapeDtypeStruct((r_out, d), F32), jax.ShapeDtypeStruct((r_out, d), F32),
                   jax.ShapeDtypeStruct((r_out, LANES), jnp.int32), jax.ShapeDtypeStruct((r_out, LANES), F32)],
        compiler_params=_cparams(("arbitrary",)),
        name="outproj_router",
    )(*args, x_all, modl, g, wo_bf, wglu_bf, bglu, wr_pad, br_pad)


def _lru_kernel(reverse, nt, x_ref, prev_ref, next_ref, cw_ref, cb_ref, w_ref, b_ref, sp_ref, h0_ref,
                h_ref, a_s, b_s, carry):
    t = pl.program_id(1)
    tt = nt - 1 - t if reverse else t
    tile = x_ref.shape[0]
    d = x_ref.shape[1]

    @pl.when(t == 0)
    def _():
        carry[...] = jnp.broadcast_to(h0_ref[...], carry.shape)

    x = x_ref[...]
    prev = jnp.where(tt == 0, 0.0, prev_ref[...])
    nxt = jnp.where(tt == nt - 1, 0.0, next_ref[...])
    ext = jnp.concatenate([prev, x, nxt], axis=0)
    c = cb_ref[...]
    for k in range(CONV_W):
        o = SUBLANES + k - CONV_LEFT
        c = c + ext[o:o + tile] * cw_ref[k:k + 1, :]
    z = jnp.dot(c.astype(BF16), w_ref[...], preferred_element_type=F32) + b_ref[...]
    r = jax.nn.sigmoid(z[:, :d])
    i = jax.nn.sigmoid(z[:, d:])
    log_a = -LRU_C * r * sp_ref[...]
    a = jnp.exp(log_a)
    b = jnp.sqrt(1.0 - a * a) * (i * c)

    row = lax.broadcasted_iota(jnp.int32, a.shape, 0) % SUBLANES
    for s in (1, 2, 4):
        if reverse:
            keep = row < SUBLANES - s
            sh = tile - s
        else:
            keep = row >= s
            sh = s
        a_sh = pltpu.roll(a, sh, 0)
        b_sh = pltpu.roll(b, sh, 0)
        b = jnp.where(keep, b + a * b_sh, b)
        a = jnp.where(keep, a * a_sh, a)
    a_s[...] = a
    b_s[...] = b
    ng = tile // SUBLANES

    def body(j, _):
        g = ng - 1 - j if reverse else j
        o = pl.multiple_of(g * SUBLANES, SUBLANES)
        h = b_s[pl.ds(o, SUBLANES), :] + a_s[pl.ds(o, SUBLANES), :] * carry[...]
        h_ref[pl.ds(o, SUBLANES), :] = h
        last = h[0:1, :] if reverse else h[SUBLANES - 1:SUBLANES, :]
        carry[...] = jnp.broadcast_to(last, carry.shape)
        return 0

    lax.fori_loop(0, ng, body, 0)


def _lru_scan(u, row0, bsz, seq, conv_w, conv_b, wcat, bcat, sp, h0, reverse, tile):
    d = u.shape[1]
    nt = seq // tile
    hb = tile // SUBLANES
    t0 = row0 // tile
    assert row0 % tile == 0

    def cur(b, t):
        return (t0 + b * nt + (nt - 1 - t if reverse else t), 0)

    def prv(b, t):
        tt = nt - 1 - t if reverse else t
        return ((t0 + b * nt) * hb + jnp.maximum(tt * hb - 1, 0), 0)

    def nxt(b, t):
        tt = nt - 1 - t if reverse else t
        return ((t0 + b * nt) * hb + jnp.minimum((tt + 1) * hb, nt * hb - 1), 0)

    def out(b, t):
        return (b * nt + (nt - 1 - t if reverse else t), 0)

    const = lambda b, t: (0, 0)
    return pl.pallas_call(
        functools.partial(_lru_kernel, reverse, nt),
        grid=(bsz, nt),
        in_specs=[
            pl.BlockSpec((tile, d), cur),
            pl.BlockSpec((SUBLANES, d), prv),
            pl.BlockSpec((SUBLANES, d), nxt),
            pl.BlockSpec((CONV_W, d), const),
            pl.BlockSpec((1, d), const),
            pl.BlockSpec((d, 2 * d), const),
            pl.BlockSpec((1, 2 * d), const),
            pl.BlockSpec((1, d), const),
            pl.BlockSpec((None, 1, d), lambda b, t: (b, 0, 0)),
        ],
        out_specs=pl.BlockSpec((tile, d), out),
        out_shape=jax.ShapeDtypeStruct((bsz * seq, d), F32),
        scratch_shapes=[pltpu.VMEM((tile, d), F32), pltpu.VMEM((tile, d), F32), pltpu.VMEM((SUBLANES, d), F32)],
        compiler_params=_cparams(("arbitrary", "arbitrary")),
        name="lru_rev" if reverse else "lru_fwd",
    )(u, u, u, conv_w, conv_b, wcat, bcat, sp, h0)


def _block_diag(w):
    n, c, _ = w.shape
    eye = jnp.eye(n, dtype=w.dtype)
    return (eye[:, None, :, None] * w[:, :, None, :]).reshape(n * c, n * c)


LRU_TILE = 512


def _lru_mixer(u_all, bsz, seq, n_ctx, conv_w, conv_b, w_a, b_a, w_x, b_x, lam):
    d = u_all.shape[1]
    hs_l, hs_c = [], []
    for dr, rev in enumerate((False, True)):
        wcat = jnp.concatenate([_block_diag(w_a[dr]), _block_diag(w_x[dr])], axis=1).astype(BF16)
        bcat = jnp.concatenate([b_a[dr], b_x[dr]])[None, :]
        sp = jax.nn.softplus(-lam[dr])[None, :]
        args = (conv_w, conv_b[None, :], wcat, bcat, sp)
        h_c = _lru_scan(u_all, bsz * seq, bsz, n_ctx, *args, jnp.zeros((bsz, 1, d), F32), rev, n_ctx)
        h_c3 = h_c.reshape(bsz, n_ctx, d)
        h_end = h_c3[:, 0:1] if rev else h_c3[:, -1:]
        hs_l.append(_lru_scan(u_all, 0, bsz, seq, *args, h_end, rev, LRU_TILE))
        hs_c.append(h_c)
    return hs_l, hs_c


S5_T = 16
S5_CHUNK_TILE = 256


def _s5_kernel(reverse, add_skip, u_ref, k1_ref, k2r_ref, k2i_ref, k3r_ref, k3i_ref, ar_ref, ai_ref, dsk_ref, h0_ref,
               y_ref, hfin_ref, s_re, s_im, hp_re, hp_im, car_re, car_im):
    t = pl.program_id(1)
    n_groups, ct, _ = u_ref.shape

    @pl.when(t == 0)
    def _():
        car_re[...] = h0_ref[0:1, :]
        car_im[...] = h0_ref[1:2, :]

    for gp in range(n_groups // 2):
        acc_r = jnp.zeros((ct, LANES), F32)
        acc_i = jnp.zeros((ct, LANES), F32)
        for g in (2 * gp, 2 * gp + 1):
            ub = u_ref[g].astype(BF16)
            acc_r = acc_r + jnp.dot(ub, k2r_ref[g], preferred_element_type=F32)
            acc_i = acc_i + jnp.dot(ub, k2i_ref[g], preferred_element_type=F32)
        s_re[:, gp * LANES:(gp + 1) * LANES] = acc_r
        s_im[:, gp * LANES:(gp + 1) * LANES] = acc_i

    ar = ar_ref[...]
    ai = ai_ref[...]

    def body(i, carry):
        hr, hi = carry
        c = ct - 1 - i if reverse else i
        hp_re[pl.ds(c, 1), :] = hr
        hp_im[pl.ds(c, 1), :] = hi
        nr = ar * hr - ai * hi + s_re[pl.ds(c, 1), :]
        ni = ar * hi + ai * hr + s_im[pl.ds(c, 1), :]
        return nr, ni

    hr, hi = lax.fori_loop(0, ct, body, (car_re[...], car_im[...]))
    car_re[...] = hr
    car_im[...] = hi
    hfin_ref[0:1, :] = hr
    hfin_ref[1:2, :] = hi

    for g in range(n_groups):
        gp = g // 2
        u = u_ref[g]
        y = jnp.dot(u.astype(BF16), k1_ref[g], preferred_element_type=F32)
        y = y + jnp.dot(hp_re[:, gp * LANES:(gp + 1) * LANES].astype(BF16), k3r_ref[g], preferred_element_type=F32)
        y = y + jnp.dot(hp_im[:, gp * LANES:(gp + 1) * LANES].astype(BF16), k3i_ref[g], preferred_element_type=F32)
        if add_skip:
            y = y + u * dsk_ref[g]
        y_ref[g] = y


def _s5_scan(ug, tabs, h0, reverse, add_skip):
    bsz, n_groups, nc, w = ug.shape
    ct = min(S5_CHUNK_TILE, nc)
    nt = nc // ct
    k1, k2r, k2i, k3r, k3i, ar, ai, dsk = tabs
    ns = ar.shape[1]
    cur = lambda b, t: (b, 0, nt - 1 - t if reverse else t, 0)
    c3 = lambda b, t: (0, 0, 0)
    c2 = lambda b, t: (0, 0)
    st = lambda b, t: (b, 0, 0)
    return pl.pallas_call(
        functools.partial(_s5_kernel, reverse, add_skip),
        grid=(bsz, nt),
        in_specs=[
            pl.BlockSpec((None, n_groups, ct, w), cur),
            pl.BlockSpec(k1.shape, c3), pl.BlockSpec(k2r.shape, c3), pl.BlockSpec(k2i.shape, c3),
            pl.BlockSpec(k3r.shape, c3), pl.BlockSpec(k3i.shape, c3),
            pl.BlockSpec(ar.shape, c2), pl.BlockSpec(ai.shape, c2), pl.BlockSpec(dsk.shape, c3),
            pl.BlockSpec((None, 2, ns), st),
        ],
        out_specs=[pl.BlockSpec((None, n_groups, ct, w), cur), pl.BlockSpec((None, 2, ns), st)],
        out_shape=[jax.ShapeDtypeStruct(ug.shape, F32), jax.ShapeDtypeStruct((bsz, 2, ns), F32)],
        scratch_shapes=[pltpu.VMEM((ct, ns), F32)] * 4 + [pltpu.VMEM((1, ns), F32)] * 2,
        compiler_params=_cparams(("arbitrary", "arbitrary")),
        name="s5_rev" if reverse else "s5_fwd",
    )(ug, k1, k2r, k2i, k3r, k3i, ar, ai, dsk, h0)


def _s5_tables(a_re, a_im, log_dt, b_re, b_im, c_re, c_im, d_skip, reverse):
    n_groups, n_state = a_re.shape
    n_ch = b_re.shape[2]
    tt = S5_T
    dt = jnp.exp(log_dt)[:, None]
    lre, lim = a_re * dt, a_im * dt
    mag = jnp.exp(lre)
    abar_r, abar_i = mag * jnp.cos(lim), mag * jnp.sin(lim)
    den = a_re * a_re + a_im * a_im
    nr = abar_r - 1.0
    coef_r = (nr * a_re + abar_i * a_im) / den
    coef_i = (abar_i * a_re - nr * a_im) / den
    bb_r = coef_r[..., None] * b_re - coef_i[..., None] * b_im
    bb_i = coef_r[..., None] * b_im + coef_i[..., None] * b_re
    dl = jnp.arange(tt + 1, dtype=F32)[:, None, None]
    pw_r = jnp.exp(dl * lre) * jnp.cos(dl * lim)
    pw_i = jnp.exp(dl * lre) * jnp.sin(dl * lim)
    cp_r = c_re[None] * pw_r[:, :, None, :] - c_im[None] * pw_i[:, :, None, :]
    cp_i = c_re[None] * pw_i[:, :, None, :] + c_im[None] * pw_r[:, :, None, :]
    m = jnp.einsum('dghp,gpk->dghk', cp_r, bb_r) - jnp.einsum('dghp,gpk->dghk', cp_i, bb_i)
    j = jnp.arange(tt)
    lag = (j[:, None] - j[None, :]) if reverse else (j[None, :] - j[:, None])
    k1 = jnp.where((lag >= 0)[:, :, None, None, None], m[jnp.clip(lag, 0, tt)], 0.0)
    k1 = k1.transpose(2, 0, 4, 1, 3).reshape(n_groups, tt * n_ch, tt * n_ch)
    e_in = j if reverse else tt - 1 - j
    k2_r = pw_r[e_in][..., None] * bb_r[None] - pw_i[e_in][..., None] * bb_i[None]
    k2_i = pw_r[e_in][..., None] * bb_i[None] + pw_i[e_in][..., None] * bb_r[None]
    e_out = tt - j if reverse else j + 1
    k3_r = cp_r[e_out].transpose(1, 3, 0, 2).reshape(n_groups, n_state, tt * n_ch)
    k3_i = -cp_i[e_out].transpose(1, 3, 0, 2).reshape(n_groups, n_state, tt * n_ch)

    def pad_cols(k):
        k = k.transpose(1, 0, 3, 2).reshape(n_groups, tt * n_ch, n_state)
        z = jnp.zeros_like(k)
        odd = (jnp.arange(n_groups) % 2 == 1)[:, None, None]
        return jnp.where(odd, jnp.concatenate([z, k], axis=2), jnp.concatenate([k, z], axis=2))

    def pad_rows(k):
        z = jnp.zeros_like(k)
        odd = (jnp.arange(n_groups) % 2 == 1)[:, None, None]
        return jnp.where(odd, jnp.concatenate([z, k], axis=1), jnp.concatenate([k, z], axis=1))

    dsk = jnp.tile(d_skip.reshape(n_groups, 1, n_ch), (1, 1, tt))
    return (k1.astype(BF16), pad_cols(k2_r).astype(BF16), pad_cols(k2_i).astype(BF16),
            pad_rows(k3_r).astype(BF16), pad_rows(k3_i).astype(BF16),
            pw_r[tt].reshape(1, -1), pw_i[tt].reshape(1, -1), dsk)


def _s5_mixer(u_all, bsz, seq, n_ctx, a_re, a_im, log_dt, b_re, b_im, c_re, c_im, d_skip):
    n_groups, n_state = a_re.shape[1], a_re.shape[2]
    n_ch = b_re.shape[3]

    def to_groups(u):
        nc = u.shape[0] // (bsz * S5_T)
        return u.reshape(bsz, nc, S5_T, n_groups, n_ch).transpose(0, 3, 1, 2, 4).reshape(bsz, n_groups, nc, S5_T * n_ch)

    def from_groups(y):
        nc = y.shape[2]
        return y.reshape(bsz, n_groups, nc, S5_T, n_ch).transpose(0, 2, 3, 1, 4).reshape(bsz * nc * S5_T, n_groups * n_ch)

    ul, uc = to_groups(u_all[:bsz * seq]), to_groups(u_all[bsz * seq:])
    zero = jnp.zeros((bsz, 2, n_groups * n_state), F32)
    y_l, y_c = None, None
    for dr, rev in enumerate((False, True)):
        tabs = _s5_tables(a_re[dr], a_im[dr], log_dt[dr], b_re[dr], b_im[dr], c_re[dr], c_im[dr], d_skip, rev)
        yc, h_end = _s5_scan(uc, tabs, zero, rev, dr == 0)
        yl, _ = _s5_scan(ul, tabs, h_end, rev, dr == 0)
        y_l = yl if y_l is None else y_l + yl
        y_c = yc if y_c is None else y_c + yc
    return from_groups(y_l), from_groups(y_c)


NA_RB = 8
NA_HALO = NA_KH // 2
_NT = (((1,), (1,)), ((), ()))


def _pair_softmax_out(h, lane, out, s_list, v_list):
    mx = None
    for s in s_list:
        m = jnp.max(s, axis=-1, keepdims=True)
        mx = m if mx is None else jnp.maximum(mx, m)
    den, o = None, None
    for s, v in zip(s_list, v_list):
        p = jnp.exp(s - mx)
        d = jnp.sum(p, axis=-1, keepdims=True)
        den = d if den is None else den + d
        pv = jnp.dot(p.astype(BF16), v, preferred_element_type=F32)
        o = pv if o is None else o + pv
    o = o / den
    return o if out is None else jnp.where(lane // NA_HD == h, o, out)


def _na_kernel(scale, q_ref, qr_ref, kp_ref, kc_ref, kn_ref, vp_ref, vc_ref, vn_ref, kx_ref, vx_ref, bias_ref, o_ref):
    lane = lax.broadcasted_iota(jnp.int32, (1, LANES), 1)
    kcat = jnp.concatenate([kp_ref[...], kc_ref[...], kn_ref[...]], axis=0)
    vcat = jnp.concatenate([vp_ref[...], vc_ref[...], vn_ref[...]], axis=0)
    q, qr, kx, vx = q_ref[...], qr_ref[...], kx_ref[...], vx_ref[...]
    out = None
    for h in range(LANES // NA_HD):
        sel = lane // NA_HD == h
        s_loc = lax.dot_general(jnp.where(sel, qr, 0).astype(BF16), kcat, _NT, preferred_element_type=F32)
        s_loc = s_loc * scale + bias_ref[h].astype(F32)
        s_ctx = lax.dot_general(jnp.where(sel, q, 0).astype(BF16), kx, _NT, preferred_element_type=F32) * scale
        out = _pair_softmax_out(h, lane, out, [s_loc, s_ctx], [vcat, vx])
    o_ref[...] = out.astype(o_ref.dtype)


def _na_bias(rpb, rows):
    n_kk = NA_RB + 2 * NA_HALO
    i = jnp.arange(NA_RB)
    kk = jnp.arange(n_kk)
    c = jnp.arange(GRID_W)
    ws = jnp.clip(c - NA_KW // 2, 0, GRID_W - NA_KW)
    col_ok = (c[None, :] >= ws[:, None]) & (c[None, :] < ws[:, None] + NA_KW)
    dc = jnp.clip(c[None, :] - c[:, None] + NA_KW - 1, 0, 2 * NA_KW - 2)
    tabs = []
    for r0 in (0, NA_RB, rows - NA_RB):
        r = r0 + i
        rs = jnp.clip(r - NA_KH // 2, 0, rows - NA_KH)
        kr = r0 - NA_HALO + kk
        row_ok = (kr[None, :] >= rs[:, None]) & (kr[None, :] < rs[:, None] + NA_KH)
        dr = jnp.clip(kr[None, :] - r[:, None] + NA_KH - 1, 0, 2 * NA_KH - 2)
        b = rpb[:, dr][:, :, :, dc]
        ok = row_ok[:, :, None, None] & col_ok[None, None]
        b = jnp.where(ok[None], b, NEG_BIG).transpose(0, 1, 3, 2, 4)
        tabs.append(b.reshape(rpb.shape[0], NA_RB * GRID_W, n_kk * GRID_W))
    return jnp.stack(tabs).astype(BF16)


def _na_lat(q, qr, k, v, bsz, seq, n_ctx, rpb):
    d_att = q.shape[1]
    rows = seq // GRID_W
    nrb = rows // NA_RB
    assert nrb >= 2 and NA_RB == 2 * NA_HALO and (bsz * seq) % n_ctx == 0
    tq = NA_RB * GRID_W
    th = NA_HALO * GRID_W
    c0 = bsz * seq // n_ctx
    bias = _na_bias(rpb, rows)
    heads_per_pair = LANES // NA_HD
    cur = lambda p, b, r: (b * nrb + r, p)
    prv = lambda p, b, r: (b * 2 * nrb + jnp.maximum(2 * r - 1, 0), p)
    nxt = lambda p, b, r: (b * 2 * nrb + jnp.minimum(2 * r + 2, 2 * nrb - 1), p)
    cx = lambda p, b, r: (c0 + b, p)
    var = lambda p, b, r: (jnp.where(r == 0, 0, jnp.where(r == nrb - 1, 2, 1)), p, 0, 0)
    bs = lambda n, f: pl.BlockSpec((n, LANES), f)
    return pl.pallas_call(
        functools.partial(_na_kernel, NA_HD ** -0.5),
        grid=(d_att // LANES, bsz, nrb),
        in_specs=[bs(tq, cur), bs(tq, cur), bs(th, prv), bs(tq, cur), bs(th, nxt), bs(th, prv), bs(tq, cur), bs(th, nxt),
                  bs(n_ctx, cx), bs(n_ctx, cx),
                  pl.BlockSpec((None, heads_per_pair, tq, tq + 2 * th), var)],
        out_specs=bs(tq, cur),
        out_shape=jax.ShapeDtypeStruct((bsz * seq, d_att), BF16),
        compiler_params=_cparams(("arbitrary", "arbitrary", "arbitrary")),
        name="na_lat",
    )(q, qr, k, k, k, v, v, v, k, v, bias)


def _na_ctx_kernel(scale, q_ref, k_ref, v_ref, o_ref):
    lane = lax.broadcasted_iota(jnp.int32, (1, LANES), 1)
    q, k, v = q_ref[...], k_ref[...], v_ref[...]
    out = None
    for h in range(LANES // NA_HD):
        s = lax.dot_general(jnp.where(lane // NA_HD == h, q, 0).astype(BF16), k, _NT, preferred_element_type=F32) * scale
        out = _pair_softmax_out(h, lane, out, [s], [v])
    o_ref[...] = out.astype(o_ref.dtype)


def _na_ctx(q, k, v, bsz, seq, n_ctx):
    d_att = q.shape[1]
    c0 = bsz * seq // n_ctx
    blk = pl.BlockSpec((n_ctx, LANES), lambda p, b: (c0 + b, p))
    return pl.pallas_call(
        functools.partial(_na_ctx_kernel, NA_HD ** -0.5),
        grid=(d_att // LANES, bsz),
        in_specs=[blk, blk, blk],
        out_specs=pl.BlockSpec((n_ctx, LANES), lambda p, b: (b, p)),
        out_shape=jax.ShapeDtypeStruct((bsz * n_ctx, d_att), BF16),
        compiler_params=_cparams(("arbitrary", "arbitrary")),
        name="na_ctx",
    )(q, k, v)


MOE_BLOCK = 256
COMBINE_TILE = 512
ROW_DMA_UNROLL = 8


def _row_dma_loop(n_rows, copy_of):
    def body(c, _):
        for u in range(ROW_DMA_UNROLL):
            copy_of(c * ROW_DMA_UNROLL + u).start()
        return 0

    lax.fori_loop(0, n_rows // ROW_DMA_UNROLL, body, 0)


def _expert_kernel(d_ff, be_ref, tok_ref, tokn_ref, rowp_ref, row_ref, f_hbm, wgu_ref, bgu_ref, wdn_ref, bdn_ref,
                   y_hbm, xbuf, ybuf, sem_in, sem_out, wgu_bf, wdn_bf):
    i = pl.program_id(0)
    n = pl.num_programs(0)
    slot = i % 2
    other = 1 - slot
    rows = xbuf.shape[1]

    def gather(idx_ref, s):
        return lambda j: pltpu.make_async_copy(f_hbm.at[pl.ds(idx_ref[0, j], 1), :], xbuf.at[s, pl.ds(j, 1), :],
                                               sem_in.at[s])

    def scatter(idx_ref, s):
        return lambda j: pltpu.make_async_copy(ybuf.at[s, pl.ds(j, 1), :], y_hbm.at[pl.ds(idx_ref[0, j], 1), :],
                                               sem_out.at[s])

    def wait_in(s):
        pltpu.make_async_copy(f_hbm.at[pl.ds(0, rows), :], xbuf.at[s], sem_in.at[s]).wait()

    def wait_out(s):
        pltpu.make_async_copy(ybuf.at[s], y_hbm.at[pl.ds(0, rows), :], sem_out.at[s]).wait()

    @pl.when(i == 0)
    def _():
        ybuf[...] = jnp.zeros_like(ybuf)
        _row_dma_loop(rows, gather(tok_ref, 0))

    wait_in(slot)

    @pl.when(i > 0)
    def _():
        wait_out(slot)

    @pl.when((i == 0) | (be_ref[i] != be_ref[jnp.maximum(i - 1, 0)]))
    def _():
        wgu_bf[...] = wgu_ref[...].astype(BF16)
        wdn_bf[...] = wdn_ref[...].astype(BF16)

    _row_dma_loop(rows, gather(tokn_ref, other))
    _row_dma_loop(rows, scatter(rowp_ref, other))
    x = xbuf[slot].astype(BF16)
    gu = jnp.dot(x, wgu_bf[...], preferred_element_type=F32) + bgu_ref[...]
    glu = jnp.minimum(gu[:, :d_ff], SWIGLU_LIMIT)
    lin = jnp.clip(gu[:, d_ff:], -SWIGLU_LIMIT, SWIGLU_LIMIT)
    act = glu * jax.nn.sigmoid(SWIGLU_ALPHA * glu) * (lin + 1.0)
    ybuf[slot] = jnp.dot(act.astype(BF16), wdn_bf[...], preferred_element_type=F32) + bdn_ref[...]

    @pl.when(i == n - 1)
    def _():
        _row_dma_loop(rows, scatter(row_ref, slot))
        wait_in(other)
        wait_out(other)
        wait_out(slot)


def _experts(f_all, blk_e, tok_buf, row_buf, n_out_rows, lyr, w_gu, b_gu, w_down, b_down):
    n_exp, d, d_gu = w_gu.shape[1:]
    d_ff = w_down.shape[2]
    nb = blk_e.shape[0]
    idspec = lambda f: pl.BlockSpec((None, 1, MOE_BLOCK), f, memory_space=pltpu.SMEM)
    wspec = lambda shp: pl.BlockSpec((None, None) + shp, lambda i, be: (lyr, be[i], 0, 0))
    return pl.pallas_call(
        functools.partial(_expert_kernel, d_ff),
        grid_spec=pltpu.PrefetchScalarGridSpec(
            num_scalar_prefetch=1,
            grid=(nb,),
            in_specs=[
                idspec(lambda i, be: (i, 0, 0)),
                idspec(lambda i, be: (jnp.minimum(i + 1, nb - 1), 0, 0)),
                idspec(lambda i, be: (jnp.where(i == 0, nb, i - 1), 0, 0)),
                idspec(lambda i, be: (i, 0, 0)),
                pl.BlockSpec(memory_space=pl.ANY),
                wspec((d, d_gu)), wspec((1, d_gu)), wspec((d_ff, d)), wspec((1, d)),
            ],
            out_specs=pl.BlockSpec(memory_space=pl.ANY),
            scratch_shapes=[pltpu.VMEM((2, MOE_BLOCK, d), F32), pltpu.VMEM((2, MOE_BLOCK, d), F32),
                            pltpu.SemaphoreType.DMA((2,)), pltpu.SemaphoreType.DMA((2,)),
                            pltpu.VMEM((d, d_gu), BF16), pltpu.VMEM((d_ff, d), BF16)],
        ),
        out_shape=jax.ShapeDtypeStruct((n_out_rows, d), F32),
        compiler_params=_cparams(("arbitrary",)),
        name="experts",
    )(blk_e, tok_buf, tok_buf, row_buf, row_buf, f_all, w_gu, b_gu.reshape(-1, n_exp, 1, d_gu), w_down,
      b_down.reshape(-1, n_exp, 1, d))


def _combine_kernel(*refs):
    y_refs, (tg_ref, x_ref, mod_ref, g_ref, o_ref) = refs[:TOP_K], refs[TOP_K:]
    tg = tg_ref[...]
    y = None
    for k in range(TOP_K):
        yk = y_refs[k][...] * tg[:, k:k + 1]
        y = yk if y is None else y + yk
    o_ref[...] = x_ref[...] + mod_ref[5:6, :] * (_rms(y) * g_ref[3:4, :])


def _combine(y4, tg, x_new, modl, g, n_tok, bsz, seq):
    tm = COMBINE_TILE
    d = x_new.shape[1]
    nt = n_tok // tm
    mod_row, _ = _row_maps(bsz, seq, tm)
    row = lambda i: (i, 0)
    return pl.pallas_call(
        _combine_kernel,
        grid=(nt,),
        in_specs=[pl.BlockSpec((tm, d), functools.partial(lambda k, i: (k * nt + i, 0), k)) for k in range(TOP_K)] + [
            pl.BlockSpec((tm, LANES), row),
            pl.BlockSpec((tm, d), row),
            pl.BlockSpec((None, N_MOD, d), lambda i: (mod_row(i), 0, 0)),
            pl.BlockSpec(g.shape, lambda i: (0, 0)),
        ],
        out_specs=pl.BlockSpec((tm, d), row),
        out_shape=jax.ShapeDtypeStruct((n_tok, d), F32),
        compiler_params=_cparams(("arbitrary",)),
        name="combine",
    )(*([y4] * TOP_K), tg, x_new, modl, g)


def _routing_tables(ti, n_tok, n_exp):
    e_flat = ti[:n_tok, :TOP_K].reshape(-1)
    n_assign = n_tok * TOP_K
    nb = -(-n_assign // MOE_BLOCK) + n_exp
    n_fill = nb * MOE_BLOCK - n_assign
    experts = jnp.arange(n_exp, dtype=jnp.int32)
    counts = (e_flat[:, None] == experts).sum(axis=0).astype(jnp.int32)
    pad_end = jnp.cumsum((-counts) % MOE_BLOCK)
    fill = jnp.arange(n_fill, dtype=jnp.int32)
    fill_e = (fill[:, None] >= pad_end[None, :]).sum(axis=1).astype(jnp.int32)
    keys = jnp.concatenate([2 * e_flat, 2 * fill_e + 1])
    ids = jnp.concatenate([jnp.arange(n_assign, dtype=jnp.int32), -1 - fill])
    keys_s, ids_s = lax.sort((keys, ids), num_keys=1, is_stable=True)
    real = ids_s >= 0
    tok_buf = jnp.where(real, ids_s // TOP_K, 0)
    spare0 = TOP_K * n_tok
    row_buf = jnp.where(real, (ids_s % TOP_K) * n_tok + ids_s // TOP_K, spare0 + (-1 - ids_s))
    row_buf = jnp.concatenate([row_buf, spare0 + n_fill + jnp.arange(MOE_BLOCK, dtype=jnp.int32)])
    blk_e = jnp.minimum(keys_s.reshape(nb, MOE_BLOCK)[:, 0] // 2, n_exp - 1)
    n_out_rows = spare0 + n_fill + MOE_BLOCK
    return blk_e, tok_buf.reshape(nb, 1, MOE_BLOCK), row_buf.reshape(nb + 1, 1, MOE_BLOCK), n_out_rows


def _moe(f_all, ti, tg, x_new, modl, g, lyr, w_gu, b_gu, w_down, b_down, n_tok, bsz, seq):
    n_exp = w_gu.shape[1]
    blk_e, tok_buf, row_buf, n_out_rows = _routing_tables(ti, n_tok, n_exp)
    y4 = _experts(f_all, blk_e, tok_buf, row_buf, n_out_rows, lyr, w_gu, b_gu, w_down, b_down)
    return _combine(y4, tg, x_new, modl, g, n_tok, bsz, seq)


def kernel(x, c, ctx, c_ctx, w_mod, b_mod, norm_g, w_in, w_out, lru_conv_w, lru_conv_b, lru_w_a, lru_b_a, lru_w_x,
           lru_b_x, lru_lambda, s5_a_re, s5_a_im, s5_log_dt, s5_b_re, s5_b_im, s5_c_re, s5_c_im, s5_d, s5_w_glu,
           s5_b_glu, na_rpb, moe_w_router, moe_b_router, moe_w_gu, moe_b_gu, moe_w_down, moe_b_down):
    bsz, seq, d = x.shape
    n_ctx = ctx.shape[1]
    depth = w_mod.shape[0]
    d_lru = lru_conv_w.shape[2]
    d_s5 = s5_d.shape[1]
    d_att = w_out.shape[1] - d_lru - d_s5
    n_exp = moe_w_router.shape[2]
    n_lat = bsz * seq
    assert bsz * n_ctx == ROW_TILE and bsz + 1 <= SUBLANES and n_exp <= LANES

    cvec = jnp.zeros((SUBLANES, d), F32).at[:bsz].set(c).at[bsz].set(c_ctx)
    mods = _modulation(cvec, w_mod, b_mod).reshape(depth, SUBLANES, N_MOD, d)
    cos_t, sin_t = _rope_tables(seq)
    x_all = jnp.concatenate([x.reshape(n_lat, d), ctx.reshape(bsz * n_ctx, d)], axis=0)

    for l in range(depth):
        need_ctx = l < depth - 1
        modl, g = mods[l], norm_g[l]
        lu, lg, su, q, qr, kr, v = _inproj(x_all, modl, g[0:1], w_in[l].astype(BF16), cos_t, sin_t,
                                           (d_lru, d_s5, d_att), bsz, seq)
        hs_l, hs_c = _lru_mixer(lu, bsz, seq, n_ctx, lru_conv_w[l], lru_conv_b[l], lru_w_a[l], lru_b_a[l],
                                lru_w_x[l], lru_b_x[l], lru_lambda[l])
        ys_l, ys_c = _s5_mixer(su, bsz, seq, n_ctx, s5_a_re[l], s5_a_im[l], s5_log_dt[l], s5_b_re[l], s5_b_im[l],
                               s5_c_re[l], s5_c_im[l], s5_d[l])
        na_l = _na_lat(q, qr, kr, v, bsz, seq, n_ctx, na_rpb[l])
        lat = (hs_l[0], hs_l[1], ys_l, na_l)
        cx = (hs_c[0], hs_c[1], ys_c, _na_ctx(q, kr, v, bsz, seq, n_ctx)) if need_ctx else None
        wr_pad = jnp.zeros((d, LANES), F32).at[:, :n_exp].set(moe_w_router[l])
        br_pad = jnp.zeros((1, LANES), F32).at[0, :n_exp].set(moe_b_router[l])
        x_new, f_all, ti, tg = _outproj(lat, cx, lg, x_all, modl, g, w_out[l].astype(BF16), s5_w_glu[l].astype(BF16),
                                        s5_b_glu[l][None, :], wr_pad, br_pad, bsz, seq, n_exp)
        n_tok = x_new.shape[0]
        x_all = _moe(f_all, ti, tg, x_new, modl, g, l, moe_w_gu, moe_b_gu, moe_w_down, moe_b_down, n_tok, bsz, seq)
    return x_all[:n_lat].reshape(bsz, seq, d)
```

```python
import functools

import jax
import jax.numpy as jnp
import numpy as np
from jax import lax
from jax.experimental import pallas as pl
from jax.experimental.pallas import tpu as pltpu

F32 = jnp.float32
BF16 = jnp.bfloat16

EPS = 1e-6
N_MOD = 6
LRU_C = 8.0
CONV_W = 4
CONV_LEFT = 2
GRID_W = 64
NA_HD = 64
NA_KH = 8
NA_KW = 16
ROPE_BASE = 10000.0
TOP_K = 4
SWIGLU_LIMIT = 7.0
SWIGLU_ALPHA = 1.702

LANES = 128
SUBLANES = 8
VMEM_LIMIT_BYTES = 56 * 1024 * 1024

NEG_BIG = -1e30


def _cparams(sem):
    return pltpu.CompilerParams(dimension_semantics=sem, vmem_limit_bytes=VMEM_LIMIT_BYTES)


def _rms(x):
    return x * lax.rsqrt(jnp.mean(x * x, axis=-1, keepdims=True) + EPS)


def _mod_kernel(c_ref, w_ref, b_ref, o_ref):
    c = c_ref[...]
    s = c * jax.nn.sigmoid(c)
    o_ref[...] = jnp.dot(s, w_ref[...], preferred_element_type=F32, precision=lax.Precision.HIGHEST) + b_ref[...]


def _modulation(cvec, w_mod, b_mod):
    n_layers, d, nd = w_mod.shape
    tn = nd // 4
    return pl.pallas_call(
        _mod_kernel,
        grid=(n_layers, nd // tn),
        in_specs=[
            pl.BlockSpec((SUBLANES, d), lambda l, j: (0, 0)),
            pl.BlockSpec((None, d, tn), lambda l, j: (l, 0, j)),
            pl.BlockSpec((None, 1, tn), lambda l, j: (l, 0, j)),
        ],
        out_specs=pl.BlockSpec((None, SUBLANES, tn), lambda l, j: (l, 0, j)),
        out_shape=jax.ShapeDtypeStruct((n_layers, SUBLANES, nd), F32),
        compiler_params=_cparams(("arbitrary", "arbitrary")),
        name="modulation",
    )(cvec, w_mod, b_mod.reshape(n_layers, 1, nd))


def _swap16(x):
    w = x.shape[-1]
    lane = lax.broadcasted_iota(jnp.int32, x.shape, 1)
    return jnp.where(lane % 32 < 16, pltpu.roll(x, w - 16, 1), pltpu.roll(x, 16, 1))


def _inproj_kernel(d_lru, d_s5, d_att, x_ref, mod_ref, g_ref, w_ref, cos_ref, sin_ref,
                   lu_ref, lg_ref, su_ref, q_ref, qr_ref, kr_ref, v_ref):
    h = _rms(x_ref[...]) * g_ref[...]
    h = h * (1.0 + mod_ref[1:2, :]) + mod_ref[0:1, :]
    y = jnp.dot(h.astype(BF16), w_ref[...], preferred_element_type=F32)
    o = 0
    lu_ref[...] = y[:, o:o + d_lru]
    o += d_lru
    lg_ref[...] = y[:, o:o + d_lru]
    o += d_lru
    for s in range(d_s5 // LANES):
        su_ref[s] = y[:, o + s * LANES:o + (s + 1) * LANES]
    o += d_s5
    q = y[:, o:o + d_att]
    o += d_att
    k = y[:, o:o + d_att]
    o += d_att
    v = y[:, o:o + d_att]
    reps = d_att // LANES
    cos = jnp.concatenate([cos_ref[...]] * reps, axis=1)
    sin = jnp.concatenate([sin_ref[...]] * reps, axis=1)
    q_ref[...] = q.astype(BF16)
    qr_ref[...] = (q * cos + _swap16(q) * sin).astype(BF16)
    kr_ref[...] = (k * cos + _swap16(k) * sin).astype(BF16)
    v_ref[...] = v.astype(BF16)


ROW_TILE = 512


def _row_maps(bsz, seq, tm):
    assert seq % tm == 0
    tps = seq // tm
    n_lat = bsz * tps
    mod_row = lambda i: jnp.where(i < n_lat, i // tps, bsz)
    rope_tile = lambda i: jnp.where(i < n_lat, i % tps, tps)
    return mod_row, rope_tile


def _inproj(x2, modl, g0, w_in_bf, cos_t, sin_t, dims, bsz, seq):
    d_lru, d_s5, d_att = dims
    r, d = x2.shape
    d_in = w_in_bf.shape[1]
    tm = ROW_TILE
    assert r % tm == 0
    mod_row, rope_tile = _row_maps(bsz, seq, tm)
    row = lambda i: (i, 0)
    n_slab = d_s5 // LANES
    outs = [jax.ShapeDtypeStruct((r, d_lru), F32)] * 2 + [jax.ShapeDtypeStruct((n_slab, r, LANES), F32)] + \
           [jax.ShapeDtypeStruct((r, d_att), BF16)] * 4
    return pl.pallas_call(
        functools.partial(_inproj_kernel, d_lru, d_s5, d_att),
        grid=(r // tm,),
        in_specs=[
            pl.BlockSpec((tm, d), row),
            pl.BlockSpec((None, N_MOD, d), lambda i: (mod_row(i), 0, 0)),
            pl.BlockSpec((1, d), lambda i: (0, 0)),
            pl.BlockSpec((d, d_in), lambda i: (0, 0)),
            pl.BlockSpec((tm, LANES), lambda i: (rope_tile(i), 0)),
            pl.BlockSpec((tm, LANES), lambda i: (rope_tile(i), 0)),
        ],
        out_specs=[pl.BlockSpec((tm, d_lru), row)] * 2 + [pl.BlockSpec((n_slab, tm, LANES), lambda i: (0, i, 0))] +
                  [pl.BlockSpec((tm, d_att), row)] * 4,
        out_shape=outs,
        compiler_params=_cparams(("arbitrary",)),
        name="inproj",
    )(x2, modl, g0, w_in_bf, cos_t, sin_t)


def _rope_tables(seq):
    nf = NA_HD // 4
    inv = ROPE_BASE ** (-jnp.arange(nf, dtype=F32) / nf)
    t = jnp.arange(seq)
    ang_r = (t // GRID_W).astype(F32)[:, None] * inv
    ang_c = (t % GRID_W).astype(F32)[:, None] * inv
    cos = jnp.concatenate([jnp.cos(ang_r)] * 2 + [jnp.cos(ang_c)] * 2, axis=1)
    sin = jnp.concatenate([-jnp.sin(ang_r), jnp.sin(ang_r), -jnp.sin(ang_c), jnp.sin(ang_c)], axis=1)
    cos = jnp.concatenate([cos, jnp.ones((ROW_TILE, NA_HD), F32)], axis=0)
    sin = jnp.concatenate([sin, jnp.zeros((ROW_TILE, NA_HD), F32)], axis=0)
    return jnp.concatenate([cos, cos], axis=1), jnp.concatenate([sin, sin], axis=1)


def _rows(ref):
    if len(ref.shape) == 2:
        return ref[...]
    return jnp.concatenate([ref[s] for s in range(ref.shape[0])], axis=1)


def _outproj_kernel(n_lat_tiles, has_ctx, n_exp, *refs):
    if has_ctx:
        (h0_ref, h1_ref, lg_ref, ys_ref, na_ref, h0c_ref, h1c_ref, ysc_ref, nac_ref,
         x_ref, mod_ref, g_ref, wo_ref, wglu_ref, bglu_ref, wr_ref, wrl_ref, br_ref, xo_ref, f_ref, ti_ref, tg_ref) = refs
        is_ctx = pl.program_id(0) >= n_lat_tiles
        pick = lambda a, c: jnp.where(is_ctx, _rows(c), _rows(a))
        h0, h1, ys, na = pick(h0_ref, h0c_ref), pick(h1_ref, h1c_ref), pick(ys_ref, ysc_ref), pick(na_ref, nac_ref)
    else:
        (h0_ref, h1_ref, lg_ref, ys_ref, na_ref,
         x_ref, mod_ref, g_ref, wo_ref, wglu_ref, bglu_ref, wr_ref, wrl_ref, br_ref, xo_ref, f_ref, ti_ref, tg_ref) = refs
        h0, h1, ys, na = _rows(h0_ref), _rows(h1_ref), _rows(ys_ref), _rows(na_ref)
    d_lru, d_s5 = h0.shape[1], ys.shape[1]
    y_lru = (h0 + h1) * jax.nn.gelu(lg_ref[...])
    z = jax.nn.gelu(ys)
    y_s5 = z * jax.nn.sigmoid(jnp.dot(z.astype(BF16), wglu_ref[...], preferred_element_type=F32) + bglu_ref[...])
    mix = jnp.dot(y_lru.astype(BF16), wo_ref[0:d_lru, :], preferred_element_type=F32)
    mix = mix + jnp.dot(y_s5.astype(BF16), wo_ref[d_lru:d_lru + d_s5, :], preferred_element_type=F32)
    mix = mix + jnp.dot(na, wo_ref[d_lru + d_s5:, :], preferred_element_type=F32)
    x_new = x_ref[...] + mod_ref[2:3, :] * (_rms(mix) * g_ref[1:2, :])
    xo_ref[...] = x_new
    f = _rms(x_new) * g_ref[2:3, :] * (1.0 + mod_ref[4:5, :]) + mod_ref[3:4, :]
    f_ref[...] = f
    f_hi = f.astype(BF16)
    f_lo = (f - f_hi.astype(F32)).astype(BF16)
    logits = jnp.dot(f_lo, wr_ref[...], preferred_element_type=F32) + jnp.dot(f_hi, wrl_ref[...], preferred_element_type=F32)
    logits = jnp.dot(f_hi, wr_ref[...], preferred_element_type=F32) + logits + br_ref[...]
    lane = lax.broadcasted_iota(jnp.int32, logits.shape, 1).astype(F32)
    logits = jnp.where(lane < n_exp, logits, -jnp.inf)
    vals, idxs = [], []
    for _ in range(TOP_K):
        m = jnp.max(logits, axis=-1, keepdims=True)
        idx = jnp.min(jnp.where(logits == m, lane, float(LANES)), axis=-1, keepdims=True)
        vals.append(m)
        idxs.append(idx)
        logits = jnp.where(lane == idx, -jnp.inf, logits)
    es = [jnp.exp(v - vals[0]) for v in vals]
    den = es[0]
    for e in es[1:]:
        den = den + e
    ti = jnp.zeros_like(lane)
    tg = jnp.zeros_like(lane)
    for k in range(TOP_K):
        ti = jnp.where(lane == k, idxs[k], ti)
        tg = jnp.where(lane == k, es[k] / den, tg)
    ti_ref[...] = ti.astype(jnp.int32)
    tg_ref[...] = tg


def _outproj(lat, ctx, lg_all, x_all, modl, g, wo_bf, wglu_bf, bglu, wr_hi, wr_lo, br_pad, bsz, seq, n_exp):
    tm = ROW_TILE
    d = x_all.shape[1]
    n_lat_tiles = bsz * seq // tm
    has_ctx = ctx is not None
    n_tiles = x_all.shape[0] // tm if has_ctx else n_lat_tiles
    mod_row, _ = _row_maps(bsz, seq, tm)
    row = lambda i: (i, 0)
    lat_row = lambda i: (jnp.minimum(i, n_lat_tiles - 1), 0)
    ctx_row = lambda i: (jnp.maximum(i - n_lat_tiles, 0), 0)
    const = lambda i: (0, 0)
    w = lambda a: pl.BlockSpec(a.shape, const)
    def rows_spec(a, f):
        if a.ndim == 2:
            return pl.BlockSpec((tm, a.shape[1]), f)
        return pl.BlockSpec((a.shape[0], tm, LANES), lambda i: (0, f(i)[0], 0))

    seq_specs = [rows_spec(a, lat_row) for a in lat]
    seq_specs.insert(2, pl.BlockSpec((tm, lg_all.shape[1]), row))
    args = [lat[0], lat[1], lg_all, lat[2], lat[3]]
    if has_ctx:
        seq_specs += [rows_spec(a, ctx_row) for a in ctx]
        args += list(ctx)
    r_out = n_tiles * tm
    return pl.pallas_call(
        functools.partial(_outproj_kernel, n_lat_tiles, has_ctx, n_exp),
        grid=(n_tiles,),
        in_specs=seq_specs + [
            pl.BlockSpec((tm, d), row),
            pl.BlockSpec((None, N_MOD, d), lambda i: (mod_row(i), 0, 0)),
            w(g), w(wo_bf), w(wglu_bf), w(bglu), w(wr_hi), w(wr_lo), w(br_pad),
        ],
        out_specs=[pl.BlockSpec((tm, d), row), pl.BlockSpec((tm, d), row),
                   pl.BlockSpec((tm, LANES), row), pl.BlockSpec((tm, LANES), row)],
        out_shape=[jax.ShapeDtypeStruct((r_out, d), F32), jax.ShapeDtypeStruct((r_out, d), F32),
                   jax.ShapeDtypeStruct((r_out, LANES), jnp.int32), jax.ShapeDtypeStruct((r_out, LANES), F32)],
        compiler_params=_cparams(("arbitrary",)),
        name="outproj_router",
    )(*args, x_all, modl, g, wo_bf, wglu_bf, bglu, wr_hi, wr_lo, br_pad)


def _lru_kernel(reverse, nt, x_ref, prev_ref, next_ref, cw_ref, cb_ref, w_ref, b_ref, sp_ref, h0_ref,
                h_ref, a_s, b_s, carry):
    t = pl.program_id(1)
    tt = nt - 1 - t if reverse else t
    tile = x_ref.shape[0]
    d = x_ref.shape[1]

    @pl.when(t == 0)
    def _():
        carry[...] = jnp.broadcast_to(h0_ref[...], carry.shape)

    x = x_ref[...]
    prev = jnp.where(tt == 0, 0.0, prev_ref[...])
    nxt = jnp.where(tt == nt - 1, 0.0, next_ref[...])
    ext = jnp.concatenate([prev, x, nxt], axis=0)
    c = cb_ref[...]
    for k in range(CONV_W):
        o = SUBLANES + k - CONV_LEFT
        c = c + ext[o:o + tile] * cw_ref[k:k + 1, :]
    z = jnp.dot(c.astype(BF16), w_ref[...], preferred_element_type=F32) + b_ref[...]
    r = jax.nn.sigmoid(z[:, :d])
    i = jax.nn.sigmoid(z[:, d:])
    log_a = -LRU_C * r * sp_ref[...]
    a = jnp.exp(log_a)
    b = jnp.sqrt(1.0 - a * a) * (i * c)

    row = lax.broadcasted_iota(jnp.int32, a.shape, 0) % SUBLANES
    for s in (1, 2, 4):
        if reverse:
            keep = row < SUBLANES - s
            sh = tile - s
        else:
            keep = row >= s
            sh = s
        a_sh = pltpu.roll(a, sh, 0)
        b_sh = pltpu.roll(b, sh, 0)
        b = jnp.where(keep, b + a * b_sh, b)
        a = jnp.where(keep, a * a_sh, a)
    a_s[...] = a
    b_s[...] = b
    ng = tile // SUBLANES

    def body(j, _):
        g = ng - 1 - j if reverse else j
        o = pl.multiple_of(g * SUBLANES, SUBLANES)
        h = b_s[pl.ds(o, SUBLANES), :] + a_s[pl.ds(o, SUBLANES), :] * carry[...]
        h_ref[pl.ds(o, SUBLANES), :] = h
        last = h[0:1, :] if reverse else h[SUBLANES - 1:SUBLANES, :]
        carry[...] = jnp.broadcast_to(last, carry.shape)
        return 0

    lax.fori_loop(0, ng, body, 0)


def _lru_scan(u, row0, bsz, seq, conv_w, conv_b, wcat, bcat, sp, h0, reverse, tile):
    d = u.shape[1]
    nt = seq // tile
    hb = tile // SUBLANES
    t0 = row0 // tile
    assert row0 % tile == 0

    def cur(b, t):
        return (t0 + b * nt + (nt - 1 - t if reverse else t), 0)

    def prv(b, t):
        tt = nt - 1 - t if reverse else t
        return ((t0 + b * nt) * hb + jnp.maximum(tt * hb - 1, 0), 0)

    def nxt(b, t):
        tt = nt - 1 - t if reverse else t
        return ((t0 + b * nt) * hb + jnp.minimum((tt + 1) * hb, nt * hb - 1), 0)

    def out(b, t):
        return (b * nt + (nt - 1 - t if reverse else t), 0)

    const = lambda b, t: (0, 0)
    return pl.pallas_call(
        functools.partial(_lru_kernel, reverse, nt),
        grid=(bsz, nt),
        in_specs=[
            pl.BlockSpec((tile, d), cur),
            pl.BlockSpec((SUBLANES, d), prv),
            pl.BlockSpec((SUBLANES, d), nxt),
            pl.BlockSpec((CONV_W, d), const),
            pl.BlockSpec((1, d), const),
            pl.BlockSpec((d, 2 * d), const),
            pl.BlockSpec((1, 2 * d), const),
            pl.BlockSpec((1, d), const),
            pl.BlockSpec((None, 1, d), lambda b, t: (b, 0, 0)),
        ],
        out_specs=pl.BlockSpec((tile, d), out),
        out_shape=jax.ShapeDtypeStruct((bsz * seq, d), F32),
        scratch_shapes=[pltpu.VMEM((tile, d), F32), pltpu.VMEM((tile, d), F32), pltpu.VMEM((SUBLANES, d), F32)],
        compiler_params=_cparams(("arbitrary", "arbitrary")),
        name="lru_rev" if reverse else "lru_fwd",
    )(u, u, u, conv_w, conv_b, wcat, bcat, sp, h0)


def _block_diag(w):
    n, a, b = w.shape[-3:]
    eye = jnp.eye(n, dtype=w.dtype)
    return (eye[:, None, :, None] * w[..., :, :, None, :]).reshape(w.shape[:-3] + (n * a, n * b))


LRU_TILE = 512


def _lru_mixer(u_all, bsz, seq, n_ctx, conv_w, conv_b, w_a, b_a, w_x, b_x, lam):
    d = u_all.shape[1]
    hs_l, hs_c = [], []
    for dr, rev in enumerate((False, True)):
        wcat = jnp.concatenate([_block_diag(w_a[dr]), _block_diag(w_x[dr])], axis=1).astype(BF16)
        bcat = jnp.concatenate([b_a[dr], b_x[dr]])[None, :]
        sp = jax.nn.softplus(-lam[dr])[None, :]
        args = (conv_w, conv_b[None, :], wcat, bcat, sp)
        h_c = _lru_scan(u_all, bsz * seq, bsz, n_ctx, *args, jnp.zeros((bsz, 1, d), F32), rev, n_ctx)
        h_c3 = h_c.reshape(bsz, n_ctx, d)
        h_end = h_c3[:, 0:1] if rev else h_c3[:, -1:]
        hs_l.append(_lru_scan(u_all, 0, bsz, seq, *args, h_end, rev, LRU_TILE))
        hs_c.append(h_c)
    return hs_l, hs_c


S5_T = 8
S5_TOKEN_TILE = 2048


def _s5_kernel(reverse, first, *refs):
    if first:
        u_ref, k1_ref, k2_ref, k3_ref, ar_ref, ai_ref, dsk_ref, h0_ref, y_ref, hfin_ref, s_buf, hp_buf, car = refs
        yp_ref = None
    else:
        u_ref, yp_ref, k1_ref, k2_ref, k3_ref, ar_ref, ai_ref, dsk_ref, h0_ref, y_ref, hfin_ref, s_buf, hp_buf, car = refs
    t = pl.program_id(1)
    n_slab, n_rows, _ = u_ref.shape
    tt = k2_ref.shape[0]
    ct = n_rows // tt
    ns = ar_ref.shape[1]

    @pl.when(t == 0)
    def _():
        car[...] = h0_ref[...]

    def tokens(ref, j):
        return jnp.concatenate([ref[s, pl.ds(j, ct, stride=tt), :] for s in range(n_slab)], axis=1)

    us = [tokens(u_ref, j) for j in range(tt)]
    ub = [u.astype(BF16) for u in us]
    acc = None
    for j in range(tt):
        p = jnp.dot(ub[j], k2_ref[j], preferred_element_type=F32)
        acc = p if acc is None else acc + p
    s_buf[...] = acc

    ar = ar_ref[...]
    ai = ai_ref[...]

    def body(i, carry):
        hr, hi = carry
        c = ct - 1 - i if reverse else i
        hp_buf[pl.ds(c, 1), 0:ns] = hr
        hp_buf[pl.ds(c, 1), ns:2 * ns] = hi
        nr = ar * hr - ai * hi + s_buf[pl.ds(c, 1), 0:ns]
        ni = ar * hi + ai * hr + s_buf[pl.ds(c, 1), ns:2 * ns]
        return nr, ni

    hr, hi = lax.fori_loop(0, ct, body, (car[0:1, :], car[1:2, :]))
    car[0:1, :] = hr
    car[1:2, :] = hi
    hfin_ref[...] = car[...]

    hpb = hp_buf[...].astype(BF16)
    for tk in range(tt):
        y = jnp.dot(hpb, k3_ref[tk], preferred_element_type=F32)
        for j in (range(tk, tt) if reverse else range(tk + 1)):
            y = y + jnp.dot(ub[j], k1_ref[abs(tk - j)], preferred_element_type=F32)
        y = y + (us[tk] * dsk_ref[...] if first else tokens(yp_ref, tk))
        for s in range(n_slab):
            y_ref[s, pl.ds(tk, ct, stride=tt), :] = y[:, s * LANES:(s + 1) * LANES]


def _s5_scan(u, y_prev, row0, bsz, seq, tabs, h0, reverse):
    n_slab = u.shape[0]
    tile = min(S5_TOKEN_TILE, seq)
    nt = seq // tile
    t0 = row0 // tile
    assert row0 % tile == 0 and seq % tile == 0
    k1, k2, k3, ar, ai, dsk = tabs
    ns = ar.shape[1]
    first = y_prev is None
    tix = lambda t: nt - 1 - t if reverse else t
    c3 = lambda b, t: (0, 0, 0)
    c2 = lambda b, t: (0, 0)
    st = lambda b, t: (b, 0, 0)
    out_blk = pl.BlockSpec((n_slab, tile, LANES), lambda b, t: (0, b * nt + tix(t), 0))
    once = dict(pipeline_mode=pl.Buffered(1))
    in_specs = [pl.BlockSpec((n_slab, tile, LANES), lambda b, t: (0, t0 + b * nt + tix(t), 0))]
    args = [u]
    if not first:
        in_specs.append(out_blk)
        args.append(y_prev)
    in_specs += [pl.BlockSpec(k1.shape, c3, **once), pl.BlockSpec(k2.shape, c3, **once),
                 pl.BlockSpec(k3.shape, c3, **once), pl.BlockSpec(ar.shape, c2), pl.BlockSpec(ai.shape, c2),
                 pl.BlockSpec(dsk.shape, c2), pl.BlockSpec((None, 2, ns), st)]
    return pl.pallas_call(
        functools.partial(_s5_kernel, reverse, first),
        grid=(bsz, nt),
        in_specs=in_specs,
        out_specs=[out_blk, pl.BlockSpec((None, 2, ns), st)],
        out_shape=[jax.ShapeDtypeStruct((n_slab, bsz * seq, LANES), F32), jax.ShapeDtypeStruct((bsz, 2, ns), F32)],
        scratch_shapes=[pltpu.VMEM((tile // S5_T, 2 * ns), F32)] * 2 + [pltpu.VMEM((2, ns), F32)],
        compiler_params=_cparams(("arbitrary", "arbitrary")),
        name="s5_rev" if reverse else "s5_fwd",
    )(*args, k1, k2, k3, ar, ai, dsk, h0)


def _s5_tables(a_re, a_im, log_dt, b_re, b_im, c_re, c_im, d_skip, reverse):
    n_groups, n_state = a_re.shape
    n_ch = b_re.shape[2]
    tt = S5_T
    dt = jnp.exp(log_dt)[:, None]
    lre, lim = a_re * dt, a_im * dt
    mag = jnp.exp(lre)
    abar_r, abar_i = mag * jnp.cos(lim), mag * jnp.sin(lim)
    den = a_re * a_re + a_im * a_im
    nr = abar_r - 1.0
    coef_r = (nr * a_re + abar_i * a_im) / den
    coef_i = (abar_i * a_re - nr * a_im) / den
    bb_r = coef_r[..., None] * b_re - coef_i[..., None] * b_im
    bb_i = coef_r[..., None] * b_im + coef_i[..., None] * b_re
    dl = jnp.arange(tt + 1, dtype=F32)[:, None, None]
    pw_r = jnp.exp(dl * lre) * jnp.cos(dl * lim)
    pw_i = jnp.exp(dl * lre) * jnp.sin(dl * lim)
    cp_r = c_re[None] * pw_r[:, :, None, :] - c_im[None] * pw_i[:, :, None, :]
    cp_i = c_re[None] * pw_i[:, :, None, :] + c_im[None] * pw_r[:, :, None, :]
    m = jnp.einsum('dghp,gpk->dghk', cp_r, bb_r) - jnp.einsum('dghp,gpk->dghk', cp_i, bb_i)
    k1 = _block_diag(m[:tt].transpose(0, 1, 3, 2))
    e_in = np.arange(tt) if reverse else tt - 1 - np.arange(tt)
    k2_r = pw_r[e_in][..., None] * bb_r[None] - pw_i[e_in][..., None] * bb_i[None]
    k2_i = pw_r[e_in][..., None] * bb_i[None] + pw_i[e_in][..., None] * bb_r[None]
    k2 = jnp.concatenate([_block_diag(k2_r.transpose(0, 1, 3, 2)), _block_diag(k2_i.transpose(0, 1, 3, 2))], axis=2)
    e_out = tt - np.arange(tt) if reverse else np.arange(tt) + 1
    k3 = jnp.concatenate([_block_diag(cp_r[e_out].transpose(0, 1, 3, 2)),
                          _block_diag(-cp_i[e_out].transpose(0, 1, 3, 2))], axis=1)
    return (k1.astype(BF16), k2.astype(BF16), k3.astype(BF16),
            pw_r[tt].reshape(1, -1), pw_i[tt].reshape(1, -1), d_skip.reshape(1, -1))


def _s5_mixer(u_all, bsz, seq, n_ctx, a_re, a_im, log_dt, b_re, b_im, c_re, c_im, d_skip):
    n_groups, n_state = a_re.shape[1], a_re.shape[2]
    zero = jnp.zeros((bsz, 2, n_groups * n_state), F32)
    y_l, y_c = None, None
    for dr, rev in enumerate((False, True)):
        tabs = _s5_tables(a_re[dr], a_im[dr], log_dt[dr], b_re[dr], b_im[dr], c_re[dr], c_im[dr], d_skip, rev)
        y_c, h_end = _s5_scan(u_all, y_c, bsz * seq, bsz, n_ctx, tabs, zero, rev)
        y_l, _ = _s5_scan(u_all, y_l, 0, bsz, seq, tabs, h_end, rev)
    return y_l, y_c


NA_RB = 8
NA_HALO = NA_KH // 2
_NT = (((1,), (1,)), ((), ()))


def _pair_softmax_out(h, lane, out, s_list, v_list):
    mx = None
    for s in s_list:
        m = jnp.max(s, axis=-1, keepdims=True)
        mx = m if mx is None else jnp.maximum(mx, m)
    den, o = None, None
    for s, v in zip(s_list, v_list):
        p = jnp.exp(s - mx)
        d = jnp.sum(p, axis=-1, keepdims=True)
        den = d if den is None else den + d
        pv = jnp.dot(p.astype(BF16), v, preferred_element_type=F32)
        o = pv if o is None else o + pv
    o = o / den
    return o if out is None else jnp.where(lane // NA_HD == h, o, out)


def _na_kernel(scale, q_ref, qr_ref, kp_ref, kc_ref, kn_ref, vp_ref, vc_ref, vn_ref, kx_ref, vx_ref, bias_ref, o_ref):
    lane = lax.broadcasted_iota(jnp.int32, (1, LANES), 1)
    kcat = jnp.concatenate([kp_ref[...], kc_ref[...], kn_ref[...]], axis=0)
    vcat = jnp.concatenate([vp_ref[...], vc_ref[...], vn_ref[...]], axis=0)
    q, qr, kx, vx = q_ref[...], qr_ref[...], kx_ref[...], vx_ref[...]
    out = None
    for h in range(LANES // NA_HD):
        sel = lane // NA_HD == h
        s_loc = lax.dot_general(jnp.where(sel, qr, 0).astype(BF16), kcat, _NT, preferred_element_type=F32)
        s_loc = s_loc * scale + bias_ref[h].astype(F32)
        s_ctx = lax.dot_general(jnp.where(sel, q, 0).astype(BF16), kx, _NT, preferred_element_type=F32) * scale
        out = _pair_softmax_out(h, lane, out, [s_loc, s_ctx], [vcat, vx])
    o_ref[...] = out.astype(o_ref.dtype)


def _na_bias(rpb, rows):
    n_kk = NA_RB + 2 * NA_HALO
    i = np.arange(NA_RB)
    kk = np.arange(n_kk)
    c = np.arange(GRID_W)
    ws = np.clip(c - NA_KW // 2, 0, GRID_W - NA_KW)
    col_ok = (c[None, :] >= ws[:, None]) & (c[None, :] < ws[:, None] + NA_KW)
    dc = c[None, :] - c[:, None] + NA_KW - 1
    sel_c = ((dc[..., None] == np.arange(2 * NA_KW - 1)) & col_ok[..., None]).astype(np.float32)
    tabs = []
    for r0 in (0, NA_RB, rows - NA_RB):
        r = r0 + i
        rs = np.clip(r - NA_KH // 2, 0, rows - NA_KH)
        kr = r0 - NA_HALO + kk
        row_ok = (kr[None, :] >= rs[:, None]) & (kr[None, :] < rs[:, None] + NA_KH)
        dr = kr[None, :] - r[:, None] + NA_KH - 1
        sel_r = ((dr[..., None] == np.arange(2 * NA_KH - 1)) & row_ok[..., None]).astype(np.float32)
        b = jnp.einsum('hab,ika,cqb->hickq', rpb, sel_r, sel_c, precision=lax.Precision.HIGHEST)
        ok = row_ok[:, None, :, None] & col_ok[None, :, None, :]
        b = jnp.where(ok[None], b, NEG_BIG)
        tabs.append(b.reshape(rpb.shape[0], NA_RB * GRID_W, n_kk * GRID_W))
    return jnp.stack(tabs).astype(BF16)


def _na_lat(q, qr, k, v, bsz, seq, n_ctx, rpb):
    d_att = q.shape[1]
    rows = seq // GRID_W
    nrb = rows // NA_RB
    assert nrb >= 2 and NA_RB == 2 * NA_HALO and (bsz * seq) % n_ctx == 0
    tq = NA_RB * GRID_W
    th = NA_HALO * GRID_W
    c0 = bsz * seq // n_ctx
    bias = _na_bias(rpb, rows)
    heads_per_pair = LANES // NA_HD
    cur = lambda p, b, r: (b * nrb + r, p)
    prv = lambda p, b, r: (b * 2 * nrb + jnp.maximum(2 * r - 1, 0), p)
    nxt = lambda p, b, r: (b * 2 * nrb + jnp.minimum(2 * r + 2, 2 * nrb - 1), p)
    cx = lambda p, b, r: (c0 + b, p)
    var = lambda p, b, r: (jnp.where(r == 0, 0, jnp.where(r == nrb - 1, 2, 1)), p, 0, 0)
    bs = lambda n, f: pl.BlockSpec((n, LANES), f)
    return pl.pallas_call(
        functools.partial(_na_kernel, NA_HD ** -0.5),
        grid=(d_att // LANES, bsz, nrb),
        in_specs=[bs(tq, cur), bs(tq, cur), bs(th, prv), bs(tq, cur), bs(th, nxt), bs(th, prv), bs(tq, cur), bs(th, nxt),
                  bs(n_ctx, cx), bs(n_ctx, cx),
                  pl.BlockSpec((None, heads_per_pair, tq, tq + 2 * th), var)],
        out_specs=bs(tq, cur),
        out_shape=jax.ShapeDtypeStruct((bsz * seq, d_att), BF16),
        compiler_params=_cparams(("arbitrary", "arbitrary", "arbitrary")),
        name="na_lat",
    )(q, qr, k, k, k, v, v, v, k, v, bias)


def _na_ctx_kernel(scale, q_ref, k_ref, v_ref, o_ref):
    lane = lax.broadcasted_iota(jnp.int32, (1, LANES), 1)
    q, k, v = q_ref[...], k_ref[...], v_ref[...]
    out = None
    for h in range(LANES // NA_HD):
        s = lax.dot_general(jnp.where(lane // NA_HD == h, q, 0).astype(BF16), k, _NT, preferred_element_type=F32) * scale
        out = _pair_softmax_out(h, lane, out, [s], [v])
    o_ref[...] = out.astype(o_ref.dtype)


def _na_ctx(q, k, v, bsz, seq, n_ctx):
    d_att = q.shape[1]
    c0 = bsz * seq // n_ctx
    blk = pl.BlockSpec((n_ctx, LANES), lambda p, b: (c0 + b, p))
    return pl.pallas_call(
        functools.partial(_na_ctx_kernel, NA_HD ** -0.5),
        grid=(d_att // LANES, bsz),
        in_specs=[blk, blk, blk],
        out_specs=pl.BlockSpec((n_ctx, LANES), lambda p, b: (b, p)),
        out_shape=jax.ShapeDtypeStruct((bsz * n_ctx, d_att), BF16),
        compiler_params=_cparams(("arbitrary", "arbitrary")),
        name="na_ctx",
    )(q, k, v)


MOE_BLOCK = 256
COMBINE_TILE = 512
ROW_DMA_UNROLL = 8


def _row_dma_loop(n_rows, copy_of):
    def body(c, _):
        for u in range(ROW_DMA_UNROLL):
            copy_of(c * ROW_DMA_UNROLL + u).start()
        return 0

    lax.fori_loop(0, n_rows // ROW_DMA_UNROLL, body, 0)


def _expert_kernel(d_ff, be_ref, tok_ref, tokn_ref, rowp_ref, row_ref, f_hbm, wgu_ref, bgu_ref, wdn_ref, bdn_ref,
                   y_hbm, xbuf, ybuf, sem_in, sem_out, wgu_bf, wdn_bf, xbf):
    i = pl.program_id(0)
    n = pl.num_programs(0)
    slot = i % 2
    other = 1 - slot
    rows = xbuf.shape[1]

    def gather(idx_ref, s):
        return lambda j: pltpu.make_async_copy(f_hbm.at[pl.ds(idx_ref[0, j], 1), :], xbuf.at[s, pl.ds(j, 1), :],
                                               sem_in.at[s])

    def scatter(idx_ref, s):
        return lambda j: pltpu.make_async_copy(ybuf.at[s, pl.ds(j, 1), :], y_hbm.at[pl.ds(idx_ref[0, j], 1), :],
                                               sem_out.at[s])

    def wait_in(s):
        pltpu.make_async_copy(f_hbm.at[pl.ds(0, rows), :], xbuf.at[s], sem_in.at[s]).wait()

    def wait_out(s):
        pltpu.make_async_copy(ybuf.at[s], y_hbm.at[pl.ds(0, rows), :], sem_out.at[s]).wait()

    @pl.when(i == 0)
    def _():
        ybuf[...] = jnp.zeros_like(ybuf)
        _row_dma_loop(rows, gather(tok_ref, 0))

    wait_in(slot)

    @pl.when(i > 0)
    def _():
        wait_out(slot)

    @pl.when((i == 0) | (be_ref[i] != be_ref[jnp.maximum(i - 1, 0)]))
    def _():
        wgu_bf[...] = wgu_ref[...].astype(BF16)
        wdn_bf[...] = wdn_ref[...].astype(BF16)

    xbf[...] = xbuf[slot].astype(BF16)
    for j in range(rows):
        gather(tokn_ref, other)(j).start()
        scatter(rowp_ref, other)(j).start()
    gu = jnp.dot(xbf[...], wgu_bf[...], preferred_element_type=F32) + bgu_ref[...]
    glu = jnp.minimum(gu[:, :d_ff], SWIGLU_LIMIT)
    lin = jnp.clip(gu[:, d_ff:], -SWIGLU_LIMIT, SWIGLU_LIMIT)
    act = glu * jax.nn.sigmoid(SWIGLU_ALPHA * glu) * (lin + 1.0)
    ybuf[slot] = jnp.dot(act.astype(BF16), wdn_bf[...], preferred_element_type=F32) + bdn_ref[...]

    @pl.when(i == n - 1)
    def _():
        _row_dma_loop(rows, scatter(row_ref, slot))
        wait_in(other)
        wait_out(other)
        wait_out(slot)


def _experts(f_all, blk_e, tok_buf, row_buf, n_out_rows, lyr, w_gu, b_gu, w_down, b_down):
    n_exp, d, d_gu = w_gu.shape[1:]
    d_ff = w_down.shape[2]
    nb = blk_e.shape[0]
    idspec = lambda f: pl.BlockSpec((None, 1, MOE_BLOCK), f, memory_space=pltpu.SMEM)
    wspec = lambda shp: pl.BlockSpec((None, None) + shp, lambda i, be: (lyr, be[i], 0, 0))
    return pl.pallas_call(
        functools.partial(_expert_kernel, d_ff),
        grid_spec=pltpu.PrefetchScalarGridSpec(
            num_scalar_prefetch=1,
            grid=(nb,),
            in_specs=[
                idspec(lambda i, be: (i, 0, 0)),
                idspec(lambda i, be: (jnp.minimum(i + 1, nb - 1), 0, 0)),
                idspec(lambda i, be: (jnp.where(i == 0, nb, i - 1), 0, 0)),
                idspec(lambda i, be: (i, 0, 0)),
                pl.BlockSpec(memory_space=pl.ANY),
                wspec((d, d_gu)), wspec((1, d_gu)), wspec((d_ff, d)), wspec((1, d)),
            ],
            out_specs=pl.BlockSpec(memory_space=pl.ANY),
            scratch_shapes=[pltpu.VMEM((2, MOE_BLOCK, d), F32), pltpu.VMEM((2, MOE_BLOCK, d), F32),
                            pltpu.SemaphoreType.DMA((2,)), pltpu.SemaphoreType.DMA((2,)),
                            pltpu.VMEM((d, d_gu), BF16), pltpu.VMEM((d_ff, d), BF16),
                            pltpu.VMEM((MOE_BLOCK, d), BF16)],
        ),
        out_shape=jax.ShapeDtypeStruct((n_out_rows, d), F32),
        compiler_params=_cparams(("arbitrary",)),
        name="experts",
    )(blk_e, tok_buf, tok_buf, row_buf, row_buf, f_all, w_gu, b_gu.reshape(-1, n_exp, 1, d_gu), w_down,
      b_down.reshape(-1, n_exp, 1, d))


def _combine_kernel(*refs):
    y_refs, (tg_ref, x_ref, mod_ref, g_ref, o_ref) = refs[:TOP_K], refs[TOP_K:]
    tg = tg_ref[...]
    y = None
    for k in range(TOP_K):
        yk = y_refs[k][...] * tg[:, k:k + 1]
        y = yk if y is None else y + yk
    o_ref[...] = x_ref[...] + mod_ref[5:6, :] * (_rms(y) * g_ref[3:4, :])


def _combine(y4, tg, x_new, modl, g, n_tok, bsz, seq):
    tm = COMBINE_TILE
    d = x_new.shape[1]
    nt = n_tok // tm
    mod_row, _ = _row_maps(bsz, seq, tm)
    row = lambda i: (i, 0)
    return pl.pallas_call(
        _combine_kernel,
        grid=(nt,),
        in_specs=[pl.BlockSpec((tm, d), functools.partial(lambda k, i: (k * nt + i, 0), k)) for k in range(TOP_K)] + [
            pl.BlockSpec((tm, LANES), row),
            pl.BlockSpec((tm, d), row),
            pl.BlockSpec((None, N_MOD, d), lambda i: (mod_row(i), 0, 0)),
            pl.BlockSpec(g.shape, lambda i: (0, 0)),
        ],
        out_specs=pl.BlockSpec((tm, d), row),
        out_shape=jax.ShapeDtypeStruct((n_tok, d), F32),
        compiler_params=_cparams(("arbitrary",)),
        name="combine",
    )(*([y4] * TOP_K), tg, x_new, modl, g)


def _routing_tables(ti, n_tok, n_exp):
    e_flat = ti[:n_tok, :TOP_K].reshape(-1)
    n_assign = n_tok * TOP_K
    nb = -(-n_assign // MOE_BLOCK) + n_exp
    n_fill = nb * MOE_BLOCK - n_assign
    experts = jnp.arange(n_exp, dtype=jnp.int32)
    counts = (e_flat[:, None] == experts).sum(axis=0).astype(jnp.int32)
    pad_end = jnp.cumsum((-counts) % MOE_BLOCK)
    fill = jnp.arange(n_fill, dtype=jnp.int32)
    fill_e = (fill[:, None] >= pad_end[None, :]).sum(axis=1).astype(jnp.int32)
    keys = jnp.concatenate([2 * e_flat, 2 * fill_e + 1])
    ids = jnp.concatenate([jnp.arange(n_assign, dtype=jnp.int32), fill])
    id_bits = max(n_assign, n_fill).bit_length()
    assert (2 * n_exp + 2) << id_bits < 2 ** 31
    packed = jnp.sort((keys << id_bits) | ids)
    keys_s, ids_s = packed >> id_bits, packed & ((1 << id_bits) - 1)
    real = keys_s % 2 == 0
    tok_buf = jnp.where(real, ids_s // TOP_K, 0)
    spare0 = TOP_K * n_tok
    row_buf = jnp.where(real, (ids_s % TOP_K) * n_tok + ids_s // TOP_K, spare0 + ids_s)
    row_buf = jnp.concatenate([row_buf, spare0 + n_fill + jnp.arange(MOE_BLOCK, dtype=jnp.int32)])
    blk_e = jnp.minimum(keys_s.reshape(nb, MOE_BLOCK)[:, 0] // 2, n_exp - 1)
    n_out_rows = spare0 + n_fill + MOE_BLOCK
    return blk_e, tok_buf.reshape(nb, 1, MOE_BLOCK), row_buf.reshape(nb + 1, 1, MOE_BLOCK), n_out_rows


def _moe(f_all, ti, tg, x_new, modl, g, lyr, w_gu, b_gu, w_down, b_down, n_tok, bsz, seq):
    n_exp = w_gu.shape[1]
    blk_e, tok_buf, row_buf, n_out_rows = _routing_tables(ti, n_tok, n_exp)
    y4 = _experts(f_all, blk_e, tok_buf, row_buf, n_out_rows, lyr, w_gu, b_gu, w_down, b_down)
    return _combine(y4, tg, x_new, modl, g, n_tok, bsz, seq)


def kernel(x, c, ctx, c_ctx, w_mod, b_mod, norm_g, w_in, w_out, lru_conv_w, lru_conv_b, lru_w_a, lru_b_a, lru_w_x,
           lru_b_x, lru_lambda, s5_a_re, s5_a_im, s5_log_dt, s5_b_re, s5_b_im, s5_c_re, s5_c_im, s5_d, s5_w_glu,
           s5_b_glu, na_rpb, moe_w_router, moe_b_router, moe_w_gu, moe_b_gu, moe_w_down, moe_b_down):
    bsz, seq, d = x.shape
    n_ctx = ctx.shape[1]
    depth = w_mod.shape[0]
    d_lru = lru_conv_w.shape[2]
    d_s5 = s5_d.shape[1]
    d_att = w_out.shape[1] - d_lru - d_s5
    n_exp = moe_w_router.shape[2]
    n_lat = bsz * seq
    assert bsz * n_ctx == ROW_TILE and bsz + 1 <= SUBLANES and n_exp <= LANES

    cvec = jnp.zeros((SUBLANES, d), F32).at[:bsz].set(c).at[bsz].set(c_ctx)
    mods = _modulation(cvec, w_mod, b_mod).reshape(depth, SUBLANES, N_MOD, d)
    cos_t, sin_t = _rope_tables(seq)
    x_all = jnp.concatenate([x.reshape(n_lat, d), ctx.reshape(bsz * n_ctx, d)], axis=0)

    for l in range(depth):
        need_ctx = l < depth - 1
        modl, g = mods[l], norm_g[l]
        lu, lg, su, q, qr, kr, v = _inproj(x_all, modl, g[0:1], w_in[l].astype(BF16), cos_t, sin_t,
                                           (d_lru, d_s5, d_att), bsz, seq)
        hs_l, hs_c = _lru_mixer(lu, bsz, seq, n_ctx, lru_conv_w[l], lru_conv_b[l], lru_w_a[l], lru_b_a[l],
                                lru_w_x[l], lru_b_x[l], lru_lambda[l])
        ys_l, ys_c = _s5_mixer(su, bsz, seq, n_ctx, s5_a_re[l], s5_a_im[l], s5_log_dt[l], s5_b_re[l], s5_b_im[l],
                               s5_c_re[l], s5_c_im[l], s5_d[l])
        na_l = _na_lat(q, qr, kr, v, bsz, seq, n_ctx, na_rpb[l])
        lat = (hs_l[0], hs_l[1], ys_l, na_l)
        cx = (hs_c[0], hs_c[1], ys_c, _na_ctx(q, kr, v, bsz, seq, n_ctx)) if need_ctx else None
        wr_pad = jnp.zeros((d, LANES), F32).at[:, :n_exp].set(moe_w_router[l])
        wr_hi = wr_pad.astype(BF16)
        wr_lo = (wr_pad - wr_hi.astype(F32)).astype(BF16)
        br_pad = jnp.zeros((1, LANES), F32).at[0, :n_exp].set(moe_b_router[l])
        x_new, f_all, ti, tg = _outproj(lat, cx, lg, x_all, modl, g, w_out[l].astype(BF16), s5_w_glu[l].astype(BF16),
                                        s5_b_glu[l][None, :], wr_hi, wr_lo, br_pad, bsz, seq, n_exp)
        n_tok = x_new.shape[0]
        x_all = _moe(f_all, ti, tg, x_new, modl, g, l, moe_w_gu, moe_b_gu, moe_w_down, moe_b_down, n_tok, bsz, seq)
    return x_all[:n_lat].reshape(bsz, seq, d)
```

```python
import functools

import jax
import jax.numpy as jnp
import numpy as np
from jax import lax
from jax.experimental import pallas as pl
from jax.experimental.pallas import tpu as pltpu

F32 = jnp.float32
BF16 = jnp.bfloat16

EPS = 1e-6
N_MOD = 6
LRU_C = 8.0
CONV_W = 4
CONV_LEFT = 2
GRID_W = 64
NA_HD = 64
NA_KH = 8
NA_KW = 16
ROPE_BASE = 10000.0
TOP_K = 4
SWIGLU_LIMIT = 7.0
SWIGLU_ALPHA = 1.702

LANES = 128
SUBLANES = 8
VMEM_LIMIT_BYTES = 56 * 1024 * 1024

NEG_BIG = -1e30


def _cparams(sem):
    return pltpu.CompilerParams(dimension_semantics=sem, vmem_limit_bytes=VMEM_LIMIT_BYTES)


def _rms(x):
    return x * lax.rsqrt(jnp.mean(x * x, axis=-1, keepdims=True) + EPS)


def _mod_kernel(c_ref, w_ref, b_ref, o_ref):
    c = c_ref[...]
    s = c * jax.nn.sigmoid(c)
    o_ref[...] = jnp.dot(s, w_ref[...], preferred_element_type=F32, precision=lax.Precision.HIGHEST) + b_ref[...]


def _modulation(cvec, w_mod, b_mod):
    n_layers, d, nd = w_mod.shape
    tn = nd // 4
    return pl.pallas_call(
        _mod_kernel,
        grid=(n_layers, nd // tn),
        in_specs=[
            pl.BlockSpec((SUBLANES, d), lambda l, j: (0, 0)),
            pl.BlockSpec((None, d, tn), lambda l, j: (l, 0, j)),
            pl.BlockSpec((None, 1, tn), lambda l, j: (l, 0, j)),
        ],
        out_specs=pl.BlockSpec((None, SUBLANES, tn), lambda l, j: (l, 0, j)),
        out_shape=jax.ShapeDtypeStruct((n_layers, SUBLANES, nd), F32),
        compiler_params=_cparams(("arbitrary", "arbitrary")),
        name="modulation",
    )(cvec, w_mod, b_mod.reshape(n_layers, 1, nd))


def _swap16(x):
    w = x.shape[-1]
    lane = lax.broadcasted_iota(jnp.int32, x.shape, 1)
    return jnp.where(lane % 32 < 16, pltpu.roll(x, w - 16, 1), pltpu.roll(x, 16, 1))


def _inproj_kernel(d_lru, d_s5, d_att, x_ref, mod_ref, g_ref, w_ref, cos_ref, sin_ref,
                   lu_ref, lg_ref, su_ref, q_ref, qr_ref, kr_ref, v_ref):
    h = _rms(x_ref[...]) * g_ref[...]
    h = h * (1.0 + mod_ref[1:2, :]) + mod_ref[0:1, :]
    y = jnp.dot(h.astype(BF16), w_ref[...], preferred_element_type=F32)
    o = 0
    lu_ref[...] = y[:, o:o + d_lru]
    o += d_lru
    lg_ref[...] = y[:, o:o + d_lru]
    o += d_lru
    for s in range(d_s5 // LANES):
        su_ref[s] = y[:, o + s * LANES:o + (s + 1) * LANES]
    o += d_s5
    q = y[:, o:o + d_att]
    o += d_att
    k = y[:, o:o + d_att]
    o += d_att
    v = y[:, o:o + d_att]
    reps = d_att // LANES
    cos = jnp.concatenate([cos_ref[...]] * reps, axis=1)
    sin = jnp.concatenate([sin_ref[...]] * reps, axis=1)
    q_ref[...] = q.astype(BF16)
    qr_ref[...] = (q * cos + _swap16(q) * sin).astype(BF16)
    kr_ref[...] = (k * cos + _swap16(k) * sin).astype(BF16)
    v_ref[...] = v.astype(BF16)


ROW_TILE = 512


def _row_maps(bsz, seq, tm):
    assert seq % tm == 0
    tps = seq // tm
    n_lat = bsz * tps
    mod_row = lambda i: jnp.where(i < n_lat, i // tps, bsz)
    rope_tile = lambda i: jnp.where(i < n_lat, i % tps, tps)
    return mod_row, rope_tile


def _inproj(x2, modl, g0, w_in_bf, cos_t, sin_t, dims, bsz, seq):
    d_lru, d_s5, d_att = dims
    r, d = x2.shape
    d_in = w_in_bf.shape[1]
    tm = ROW_TILE
    assert r % tm == 0
    mod_row, rope_tile = _row_maps(bsz, seq, tm)
    row = lambda i: (i, 0)
    n_slab = d_s5 // LANES
    outs = [jax.ShapeDtypeStruct((r, d_lru), F32)] * 2 + [jax.ShapeDtypeStruct((n_slab, r, LANES), F32)] + \
           [jax.ShapeDtypeStruct((r, d_att), BF16)] * 4
    return pl.pallas_call(
        functools.partial(_inproj_kernel, d_lru, d_s5, d_att),
        grid=(r // tm,),
        in_specs=[
            pl.BlockSpec((tm, d), row),
            pl.BlockSpec((None, N_MOD, d), lambda i: (mod_row(i), 0, 0)),
            pl.BlockSpec((1, d), lambda i: (0, 0)),
            pl.BlockSpec((d, d_in), lambda i: (0, 0)),
            pl.BlockSpec((tm, LANES), lambda i: (rope_tile(i), 0)),
            pl.BlockSpec((tm, LANES), lambda i: (rope_tile(i), 0)),
        ],
        out_specs=[pl.BlockSpec((tm, d_lru), row)] * 2 + [pl.BlockSpec((n_slab, tm, LANES), lambda i: (0, i, 0))] +
                  [pl.BlockSpec((tm, d_att), row)] * 4,
        out_shape=outs,
        compiler_params=_cparams(("arbitrary",)),
        name="inproj",
    )(x2, modl, g0, w_in_bf, cos_t, sin_t)


def _rope_tables(seq):
    nf = NA_HD // 4
    inv = ROPE_BASE ** (-jnp.arange(nf, dtype=F32) / nf)
    t = jnp.arange(seq)
    ang_r = (t // GRID_W).astype(F32)[:, None] * inv
    ang_c = (t % GRID_W).astype(F32)[:, None] * inv
    cos = jnp.concatenate([jnp.cos(ang_r)] * 2 + [jnp.cos(ang_c)] * 2, axis=1)
    sin = jnp.concatenate([-jnp.sin(ang_r), jnp.sin(ang_r), -jnp.sin(ang_c), jnp.sin(ang_c)], axis=1)
    cos = jnp.concatenate([cos, jnp.ones((ROW_TILE, NA_HD), F32)], axis=0)
    sin = jnp.concatenate([sin, jnp.zeros((ROW_TILE, NA_HD), F32)], axis=0)
    return jnp.concatenate([cos, cos], axis=1), jnp.concatenate([sin, sin], axis=1)


def _store_token_tiles(ref, v):
    n = v.shape[1] // LANES
    for s in range(n):
        ref[pl.ds(s, v.shape[0], stride=n), :] = v[:, s * LANES:(s + 1) * LANES]


def _load_token_tiles(ref, n):
    rows = ref.shape[0] // n
    return jnp.concatenate([ref[pl.ds(s, rows, stride=n), :] for s in range(n)], axis=1)


def _rows(ref):
    if len(ref.shape) == 2:
        return ref[...]
    return jnp.concatenate([ref[s] for s in range(ref.shape[0])], axis=1)


def _outproj_kernel(n_lat_tiles, has_ctx, n_exp, *refs):
    if has_ctx:
        (h0_ref, h1_ref, lg_ref, ys_ref, na_ref, h0c_ref, h1c_ref, ysc_ref, nac_ref,
         x_ref, mod_ref, g_ref, wo_ref, wglu_ref, bglu_ref, wr_ref, wrl_ref, br_ref, xo_ref, f_ref, ti_ref, tg_ref) = refs
        is_ctx = pl.program_id(0) >= n_lat_tiles
        pick = lambda a, c: jnp.where(is_ctx, _rows(c), _rows(a))
        h0, h1, ys, na = pick(h0_ref, h0c_ref), pick(h1_ref, h1c_ref), pick(ys_ref, ysc_ref), pick(na_ref, nac_ref)
    else:
        (h0_ref, h1_ref, lg_ref, ys_ref, na_ref,
         x_ref, mod_ref, g_ref, wo_ref, wglu_ref, bglu_ref, wr_ref, wrl_ref, br_ref, xo_ref, f_ref, ti_ref, tg_ref) = refs
        h0, h1, ys, na = _rows(h0_ref), _rows(h1_ref), _rows(ys_ref), _rows(na_ref)
    d_lru, d_s5 = h0.shape[1], ys.shape[1]
    y_lru = (h0 + h1) * jax.nn.gelu(lg_ref[...])
    z = jax.nn.gelu(ys)
    y_s5 = z * jax.nn.sigmoid(jnp.dot(z.astype(BF16), wglu_ref[...], preferred_element_type=F32) + bglu_ref[...])
    mix = jnp.dot(y_lru.astype(BF16), wo_ref[0:d_lru, :], preferred_element_type=F32)
    mix = mix + jnp.dot(y_s5.astype(BF16), wo_ref[d_lru:d_lru + d_s5, :], preferred_element_type=F32)
    mix = mix + jnp.dot(na, wo_ref[d_lru + d_s5:, :], preferred_element_type=F32)
    x_new = x_ref[...] + mod_ref[2:3, :] * (_rms(mix) * g_ref[1:2, :])
    xo_ref[...] = x_new
    f = _rms(x_new) * g_ref[2:3, :] * (1.0 + mod_ref[4:5, :]) + mod_ref[3:4, :]
    _store_token_tiles(f_ref, f)
    f_hi = f.astype(BF16)
    f_lo = (f - f_hi.astype(F32)).astype(BF16)
    logits = jnp.dot(f_lo, wr_ref[...], preferred_element_type=F32) + jnp.dot(f_hi, wrl_ref[...], preferred_element_type=F32)
    logits = jnp.dot(f_hi, wr_ref[...], preferred_element_type=F32) + logits + br_ref[...]
    lane = lax.broadcasted_iota(jnp.int32, logits.shape, 1).astype(F32)
    logits = jnp.where(lane < n_exp, logits, -jnp.inf)
    vals, idxs = [], []
    for _ in range(TOP_K):
        m = jnp.max(logits, axis=-1, keepdims=True)
        idx = jnp.min(jnp.where(logits == m, lane, float(LANES)), axis=-1, keepdims=True)
        vals.append(m)
        idxs.append(idx)
        logits = jnp.where(lane == idx, -jnp.inf, logits)
    es = [jnp.exp(v - vals[0]) for v in vals]
    den = es[0]
    for e in es[1:]:
        den = den + e
    ti = jnp.zeros_like(lane)
    tg = jnp.zeros_like(lane)
    for k in range(TOP_K):
        ti = jnp.where(lane == k, idxs[k], ti)
        tg = jnp.where(lane == k, es[k] / den, tg)
    ti_ref[...] = ti.astype(jnp.int32)
    tg_ref[...] = tg


def _outproj(lat, ctx, lg_all, x_all, modl, g, wo_bf, wglu_bf, bglu, wr_hi, wr_lo, br_pad, bsz, seq, n_exp):
    tm = ROW_TILE
    d = x_all.shape[1]
    n_lat_tiles = bsz * seq // tm
    has_ctx = ctx is not None
    n_tiles = x_all.shape[0] // tm if has_ctx else n_lat_tiles
    mod_row, _ = _row_maps(bsz, seq, tm)
    row = lambda i: (i, 0)
    lat_row = lambda i: (jnp.minimum(i, n_lat_tiles - 1), 0)
    ctx_row = lambda i: (jnp.maximum(i - n_lat_tiles, 0), 0)
    const = lambda i: (0, 0)
    w = lambda a: pl.BlockSpec(a.shape, const)
    def rows_spec(a, f):
        if a.ndim == 2:
            return pl.BlockSpec((tm, a.shape[1]), f)
        return pl.BlockSpec((a.shape[0], tm, LANES), lambda i: (0, f(i)[0], 0))

    seq_specs = [rows_spec(a, lat_row) for a in lat]
    seq_specs.insert(2, pl.BlockSpec((tm, lg_all.shape[1]), row))
    args = [lat[0], lat[1], lg_all, lat[2], lat[3]]
    if has_ctx:
        seq_specs += [rows_spec(a, ctx_row) for a in ctx]
        args += list(ctx)
    r_out = n_tiles * tm
    return pl.pallas_call(
        functools.partial(_outproj_kernel, n_lat_tiles, has_ctx, n_exp),
        grid=(n_tiles,),
        in_specs=seq_specs + [
            pl.BlockSpec((tm, d), row),
            pl.BlockSpec((None, N_MOD, d), lambda i: (mod_row(i), 0, 0)),
            w(g), w(wo_bf), w(wglu_bf), w(bglu), w(wr_hi), w(wr_lo), w(br_pad),
        ],
        out_specs=[pl.BlockSpec((tm, d), row), pl.BlockSpec((tm * d // LANES, LANES), row),
                   pl.BlockSpec((tm, LANES), row), pl.BlockSpec((tm, LANES), row)],
        out_shape=[jax.ShapeDtypeStruct((r_out, d), F32), jax.ShapeDtypeStruct((r_out * d // LANES, LANES), F32),
                   jax.ShapeDtypeStruct((r_out, LANES), jnp.int32), jax.ShapeDtypeStruct((r_out, LANES), F32)],
        compiler_params=_cparams(("arbitrary",)),
        name="outproj_router",
    )(*args, x_all, modl, g, wo_bf, wglu_bf, bglu, wr_hi, wr_lo, br_pad)


def _lru_kernel(reverse, nt, x_ref, prev_ref, next_ref, cw_ref, cb_ref, w_ref, b_ref, sp_ref, h0_ref,
                h_ref, a_s, b_s, carry):
    t = pl.program_id(1)
    tt = nt - 1 - t if reverse else t
    tile = x_ref.shape[0]
    d = x_ref.shape[1]

    @pl.when(t == 0)
    def _():
        carry[...] = jnp.broadcast_to(h0_ref[...], carry.shape)

    x = x_ref[...]
    prev = jnp.where(tt == 0, 0.0, prev_ref[...])
    nxt = jnp.where(tt == nt - 1, 0.0, next_ref[...])
    ext = jnp.concatenate([prev, x, nxt], axis=0)
    c = cb_ref[...]
    for k in range(CONV_W):
        o = SUBLANES + k - CONV_LEFT
        c = c + ext[o:o + tile] * cw_ref[k:k + 1, :]
    z = jnp.dot(c.astype(BF16), w_ref[...], preferred_element_type=F32) + b_ref[...]
    r = jax.nn.sigmoid(z[:, :d])
    i = jax.nn.sigmoid(z[:, d:])
    log_a = -LRU_C * r * sp_ref[...]
    a = jnp.exp(log_a)
    b = jnp.sqrt(1.0 - a * a) * (i * c)

    row = lax.broadcasted_iota(jnp.int32, a.shape, 0) % SUBLANES
    for s in (1, 2, 4):
        if reverse:
            keep = row < SUBLANES - s
            sh = tile - s
        else:
            keep = row >= s
            sh = s
        a_sh = pltpu.roll(a, sh, 0)
        b_sh = pltpu.roll(b, sh, 0)
        b = jnp.where(keep, b + a * b_sh, b)
        a = jnp.where(keep, a * a_sh, a)
    a_s[...] = a
    b_s[...] = b
    ng = tile // SUBLANES

    def body(j, _):
        g = ng - 1 - j if reverse else j
        o = pl.multiple_of(g * SUBLANES, SUBLANES)
        h = b_s[pl.ds(o, SUBLANES), :] + a_s[pl.ds(o, SUBLANES), :] * carry[...]
        h_ref[pl.ds(o, SUBLANES), :] = h
        last = h[0:1, :] if reverse else h[SUBLANES - 1:SUBLANES, :]
        carry[...] = jnp.broadcast_to(last, carry.shape)
        return 0

    lax.fori_loop(0, ng, body, 0)


def _lru_scan(u, row0, bsz, seq, conv_w, conv_b, wcat, bcat, sp, h0, reverse, tile):
    d = u.shape[1]
    nt = seq // tile
    hb = tile // SUBLANES
    t0 = row0 // tile
    assert row0 % tile == 0

    def cur(b, t):
        return (t0 + b * nt + (nt - 1 - t if reverse else t), 0)

    def prv(b, t):
        tt = nt - 1 - t if reverse else t
        return ((t0 + b * nt) * hb + jnp.maximum(tt * hb - 1, 0), 0)

    def nxt(b, t):
        tt = nt - 1 - t if reverse else t
        return ((t0 + b * nt) * hb + jnp.minimum((tt + 1) * hb, nt * hb - 1), 0)

    def out(b, t):
        return (b * nt + (nt - 1 - t if reverse else t), 0)

    const = lambda b, t: (0, 0)
    return pl.pallas_call(
        functools.partial(_lru_kernel, reverse, nt),
        grid=(bsz, nt),
        in_specs=[
            pl.BlockSpec((tile, d), cur),
            pl.BlockSpec((SUBLANES, d), prv),
            pl.BlockSpec((SUBLANES, d), nxt),
            pl.BlockSpec((CONV_W, d), const),
            pl.BlockSpec((1, d), const),
            pl.BlockSpec((d, 2 * d), const),
            pl.BlockSpec((1, 2 * d), const),
            pl.BlockSpec((1, d), const),
            pl.BlockSpec((None, 1, d), lambda b, t: (b, 0, 0)),
        ],
        out_specs=pl.BlockSpec((tile, d), out),
        out_shape=jax.ShapeDtypeStruct((bsz * seq, d), F32),
        scratch_shapes=[pltpu.VMEM((tile, d), F32), pltpu.VMEM((tile, d), F32), pltpu.VMEM((SUBLANES, d), F32)],
        compiler_params=_cparams(("arbitrary", "arbitrary")),
        name="lru_rev" if reverse else "lru_fwd",
    )(u, u, u, conv_w, conv_b, wcat, bcat, sp, h0)


def _block_diag(w):
    n, a, b = w.shape[-3:]
    eye = jnp.eye(n, dtype=w.dtype)
    return (eye[:, None, :, None] * w[..., :, :, None, :]).reshape(w.shape[:-3] + (n * a, n * b))


LRU_TILE = 512


def _lru_mixer(u_all, bsz, seq, n_ctx, conv_w, conv_b, w_a, b_a, w_x, b_x, lam):
    d = u_all.shape[1]
    hs_l, hs_c = [], []
    for dr, rev in enumerate((False, True)):
        wcat = jnp.concatenate([_block_diag(w_a[dr]), _block_diag(w_x[dr])], axis=1).astype(BF16)
        bcat = jnp.concatenate([b_a[dr], b_x[dr]])[None, :]
        sp = jax.nn.softplus(-lam[dr])[None, :]
        args = (conv_w, conv_b[None, :], wcat, bcat, sp)
        h_c = _lru_scan(u_all, bsz * seq, bsz, n_ctx, *args, jnp.zeros((bsz, 1, d), F32), rev, n_ctx)
        h_c3 = h_c.reshape(bsz, n_ctx, d)
        h_end = h_c3[:, 0:1] if rev else h_c3[:, -1:]
        hs_l.append(_lru_scan(u_all, 0, bsz, seq, *args, h_end, rev, LRU_TILE))
        hs_c.append(h_c)
    return hs_l, hs_c


S5_T = 8
S5_TOKEN_TILE = 2048


def _s5_kernel(reverse, first, *refs):
    if first:
        u_ref, k1_ref, k2_ref, k3_ref, ar_ref, ai_ref, dsk_ref, h0_ref, y_ref, hfin_ref, s_buf, hp_buf, car = refs
        yp_ref = None
    else:
        u_ref, yp_ref, k1_ref, k2_ref, k3_ref, ar_ref, ai_ref, dsk_ref, h0_ref, y_ref, hfin_ref, s_buf, hp_buf, car = refs
    t = pl.program_id(1)
    n_slab, n_rows, _ = u_ref.shape
    tt = k2_ref.shape[0]
    ct = n_rows // tt
    ns = ar_ref.shape[1]

    @pl.when(t == 0)
    def _():
        car[...] = h0_ref[...]

    def tokens(ref, j):
        return jnp.concatenate([ref[s, pl.ds(j, ct, stride=tt), :] for s in range(n_slab)], axis=1)

    us = [tokens(u_ref, j) for j in range(tt)]
    ub = [u.astype(BF16) for u in us]
    acc = None
    for j in range(tt):
        p = jnp.dot(ub[j], k2_ref[j], preferred_element_type=F32)
        acc = p if acc is None else acc + p
    s_buf[...] = acc

    ar = ar_ref[...]
    ai = ai_ref[...]

    def body(i, carry):
        hr, hi = carry
        c = ct - 1 - i if reverse else i
        hp_buf[pl.ds(c, 1), 0:ns] = hr
        hp_buf[pl.ds(c, 1), ns:2 * ns] = hi
        nr = ar * hr - ai * hi + s_buf[pl.ds(c, 1), 0:ns]
        ni = ar * hi + ai * hr + s_buf[pl.ds(c, 1), ns:2 * ns]
        return nr, ni

    hr, hi = lax.fori_loop(0, ct, body, (car[0:1, :], car[1:2, :]))
    car[0:1, :] = hr
    car[1:2, :] = hi
    hfin_ref[...] = car[...]

    hpb = hp_buf[...].astype(BF16)
    for tk in range(tt):
        y = jnp.dot(hpb, k3_ref[tk], preferred_element_type=F32)
        for j in (range(tk, tt) if reverse else range(tk + 1)):
            y = y + jnp.dot(ub[j], k1_ref[abs(tk - j)], preferred_element_type=F32)
        y = y + (us[tk] * dsk_ref[...] if first else tokens(yp_ref, tk))
        for s in range(n_slab):
            y_ref[s, pl.ds(tk, ct, stride=tt), :] = y[:, s * LANES:(s + 1) * LANES]


def _s5_scan(u, y_prev, row0, bsz, seq, tabs, h0, reverse):
    n_slab = u.shape[0]
    tile = min(S5_TOKEN_TILE, seq)
    nt = seq // tile
    t0 = row0 // tile
    assert row0 % tile == 0 and seq % tile == 0
    k1, k2, k3, ar, ai, dsk = tabs
    ns = ar.shape[1]
    first = y_prev is None
    tix = lambda t: nt - 1 - t if reverse else t
    c3 = lambda b, t: (0, 0, 0)
    c2 = lambda b, t: (0, 0)
    st = lambda b, t: (b, 0, 0)
    out_blk = pl.BlockSpec((n_slab, tile, LANES), lambda b, t: (0, b * nt + tix(t), 0))
    once = dict(pipeline_mode=pl.Buffered(1))
    in_specs = [pl.BlockSpec((n_slab, tile, LANES), lambda b, t: (0, t0 + b * nt + tix(t), 0))]
    args = [u]
    if not first:
        in_specs.append(out_blk)
        args.append(y_prev)
    in_specs += [pl.BlockSpec(k1.shape, c3, **once), pl.BlockSpec(k2.shape, c3, **once),
                 pl.BlockSpec(k3.shape, c3, **once), pl.BlockSpec(ar.shape, c2), pl.BlockSpec(ai.shape, c2),
                 pl.BlockSpec(dsk.shape, c2), pl.BlockSpec((None, 2, ns), st)]
    return pl.pallas_call(
        functools.partial(_s5_kernel, reverse, first),
        grid=(bsz, nt),
        in_specs=in_specs,
        out_specs=[out_blk, pl.BlockSpec((None, 2, ns), st)],
        out_shape=[jax.ShapeDtypeStruct((n_slab, bsz * seq, LANES), F32), jax.ShapeDtypeStruct((bsz, 2, ns), F32)],
        scratch_shapes=[pltpu.VMEM((tile // S5_T, 2 * ns), F32)] * 2 + [pltpu.VMEM((2, ns), F32)],
        compiler_params=_cparams(("arbitrary", "arbitrary")),
        name="s5_rev" if reverse else "s5_fwd",
    )(*args, k1, k2, k3, ar, ai, dsk, h0)


def _s5_tables(a_re, a_im, log_dt, b_re, b_im, c_re, c_im, d_skip, reverse):
    n_groups, n_state = a_re.shape
    n_ch = b_re.shape[2]
    tt = S5_T
    dt = jnp.exp(log_dt)[:, None]
    lre, lim = a_re * dt, a_im * dt
    mag = jnp.exp(lre)
    abar_r, abar_i = mag * jnp.cos(lim), mag * jnp.sin(lim)
    den = a_re * a_re + a_im * a_im
    nr = abar_r - 1.0
    coef_r = (nr * a_re + abar_i * a_im) / den
    coef_i = (abar_i * a_re - nr * a_im) / den
    bb_r = coef_r[..., None] * b_re - coef_i[..., None] * b_im
    bb_i = coef_r[..., None] * b_im + coef_i[..., None] * b_re
    dl = jnp.arange(tt + 1, dtype=F32)[:, None, None]
    pw_r = jnp.exp(dl * lre) * jnp.cos(dl * lim)
    pw_i = jnp.exp(dl * lre) * jnp.sin(dl * lim)
    cp_r = c_re[None] * pw_r[:, :, None, :] - c_im[None] * pw_i[:, :, None, :]
    cp_i = c_re[None] * pw_i[:, :, None, :] + c_im[None] * pw_r[:, :, None, :]
    m = jnp.einsum('dghp,gpk->dghk', cp_r, bb_r) - jnp.einsum('dghp,gpk->dghk', cp_i, bb_i)
    k1 = _block_diag(m[:tt].transpose(0, 1, 3, 2))
    e_in = np.arange(tt) if reverse else tt - 1 - np.arange(tt)
    k2_r = pw_r[e_in][..., None] * bb_r[None] - pw_i[e_in][..., None] * bb_i[None]
    k2_i = pw_r[e_in][..., None] * bb_i[None] + pw_i[e_in][..., None] * bb_r[None]
    k2 = jnp.concatenate([_block_diag(k2_r.transpose(0, 1, 3, 2)), _block_diag(k2_i.transpose(0, 1, 3, 2))], axis=2)
    e_out = tt - np.arange(tt) if reverse else np.arange(tt) + 1
    k3 = jnp.concatenate([_block_diag(cp_r[e_out].transpose(0, 1, 3, 2)),
                          _block_diag(-cp_i[e_out].transpose(0, 1, 3, 2))], axis=1)
    return (k1.astype(BF16), k2.astype(BF16), k3.astype(BF16),
            pw_r[tt].reshape(1, -1), pw_i[tt].reshape(1, -1), d_skip.reshape(1, -1))


def _s5_mixer(u_all, bsz, seq, n_ctx, a_re, a_im, log_dt, b_re, b_im, c_re, c_im, d_skip):
    n_groups, n_state = a_re.shape[1], a_re.shape[2]
    zero = jnp.zeros((bsz, 2, n_groups * n_state), F32)
    y_l, y_c = None, None
    for dr, rev in enumerate((False, True)):
        tabs = _s5_tables(a_re[dr], a_im[dr], log_dt[dr], b_re[dr], b_im[dr], c_re[dr], c_im[dr], d_skip, rev)
        y_c, h_end = _s5_scan(u_all, y_c, bsz * seq, bsz, n_ctx, tabs, zero, rev)
        y_l, _ = _s5_scan(u_all, y_l, 0, bsz, seq, tabs, h_end, rev)
    return y_l, y_c


NA_RB = 8
NA_HALO = NA_KH // 2
_NT = (((1,), (1,)), ((), ()))


def _pair_softmax_out(h, lane, out, s_list, v_list):
    mx = None
    for s in s_list:
        m = jnp.max(s, axis=-1, keepdims=True)
        mx = m if mx is None else jnp.maximum(mx, m)
    den, o = None, None
    for s, v in zip(s_list, v_list):
        p = jnp.exp(s - mx)
        d = jnp.sum(p, axis=-1, keepdims=True)
        den = d if den is None else den + d
        pv = jnp.dot(p.astype(BF16), v, preferred_element_type=F32)
        o = pv if o is None else o + pv
    o = o / den
    return o if out is None else jnp.where(lane // NA_HD == h, o, out)


def _na_kernel(scale, q_ref, qr_ref, kp_ref, kc_ref, kn_ref, vp_ref, vc_ref, vn_ref, kx_ref, vx_ref, bias_ref, o_ref):
    lane = lax.broadcasted_iota(jnp.int32, (1, LANES), 1)
    kcat = jnp.concatenate([kp_ref[...], kc_ref[...], kn_ref[...]], axis=0)
    vcat = jnp.concatenate([vp_ref[...], vc_ref[...], vn_ref[...]], axis=0)
    q, qr, kx, vx = q_ref[...], qr_ref[...], kx_ref[...], vx_ref[...]
    out = None
    for h in range(LANES // NA_HD):
        sel = lane // NA_HD == h
        s_loc = lax.dot_general(jnp.where(sel, qr, 0).astype(BF16), kcat, _NT, preferred_element_type=F32)
        s_loc = s_loc * scale + bias_ref[h].astype(F32)
        s_ctx = lax.dot_general(jnp.where(sel, q, 0).astype(BF16), kx, _NT, preferred_element_type=F32) * scale
        out = _pair_softmax_out(h, lane, out, [s_loc, s_ctx], [vcat, vx])
    o_ref[...] = out.astype(o_ref.dtype)


def _na_bias(rpb, rows):
    n_kk = NA_RB + 2 * NA_HALO
    i = np.arange(NA_RB)
    kk = np.arange(n_kk)
    c = np.arange(GRID_W)
    ws = np.clip(c - NA_KW // 2, 0, GRID_W - NA_KW)
    col_ok = (c[None, :] >= ws[:, None]) & (c[None, :] < ws[:, None] + NA_KW)
    dc = c[None, :] - c[:, None] + NA_KW - 1
    sel_c = ((dc[..., None] == np.arange(2 * NA_KW - 1)) & col_ok[..., None]).astype(np.float32)
    tabs = []
    for r0 in (0, NA_RB, rows - NA_RB):
        r = r0 + i
        rs = np.clip(r - NA_KH // 2, 0, rows - NA_KH)
        kr = r0 - NA_HALO + kk
        row_ok = (kr[None, :] >= rs[:, None]) & (kr[None, :] < rs[:, None] + NA_KH)
        dr = kr[None, :] - r[:, None] + NA_KH - 1
        sel_r = ((dr[..., None] == np.arange(2 * NA_KH - 1)) & row_ok[..., None]).astype(np.float32)
        b = jnp.einsum('hab,ika,cqb->hickq', rpb, sel_r, sel_c, precision=lax.Precision.HIGHEST)
        ok = row_ok[:, None, :, None] & col_ok[None, :, None, :]
        b = jnp.where(ok[None], b, NEG_BIG)
        tabs.append(b.reshape(rpb.shape[0], NA_RB * GRID_W, n_kk * GRID_W))
    return jnp.stack(tabs).astype(BF16)


def _na_lat(q, qr, k, v, bsz, seq, n_ctx, rpb):
    d_att = q.shape[1]
    rows = seq // GRID_W
    nrb = rows // NA_RB
    assert nrb >= 2 and NA_RB == 2 * NA_HALO and (bsz * seq) % n_ctx == 0
    tq = NA_RB * GRID_W
    th = NA_HALO * GRID_W
    c0 = bsz * seq // n_ctx
    bias = _na_bias(rpb, rows)
    heads_per_pair = LANES // NA_HD
    cur = lambda p, b, r: (b * nrb + r, p)
    prv = lambda p, b, r: (b * 2 * nrb + jnp.maximum(2 * r - 1, 0), p)
    nxt = lambda p, b, r: (b * 2 * nrb + jnp.minimum(2 * r + 2, 2 * nrb - 1), p)
    cx = lambda p, b, r: (c0 + b, p)
    var = lambda p, b, r: (jnp.where(r == 0, 0, jnp.where(r == nrb - 1, 2, 1)), p, 0, 0)
    bs = lambda n, f: pl.BlockSpec((n, LANES), f)
    return pl.pallas_call(
        functools.partial(_na_kernel, NA_HD ** -0.5),
        grid=(d_att // LANES, bsz, nrb),
        in_specs=[bs(tq, cur), bs(tq, cur), bs(th, prv), bs(tq, cur), bs(th, nxt), bs(th, prv), bs(tq, cur), bs(th, nxt),
                  bs(n_ctx, cx), bs(n_ctx, cx),
                  pl.BlockSpec((None, heads_per_pair, tq, tq + 2 * th), var)],
        out_specs=bs(tq, cur),
        out_shape=jax.ShapeDtypeStruct((bsz * seq, d_att), BF16),
        compiler_params=_cparams(("arbitrary", "arbitrary", "arbitrary")),
        name="na_lat",
    )(q, qr, k, k, k, v, v, v, k, v, bias)


def _na_ctx_kernel(scale, q_ref, k_ref, v_ref, o_ref):
    lane = lax.broadcasted_iota(jnp.int32, (1, LANES), 1)
    q, k, v = q_ref[...], k_ref[...], v_ref[...]
    out = None
    for h in range(LANES // NA_HD):
        s = lax.dot_general(jnp.where(lane // NA_HD == h, q, 0).astype(BF16), k, _NT, preferred_element_type=F32) * scale
        out = _pair_softmax_out(h, lane, out, [s], [v])
    o_ref[...] = out.astype(o_ref.dtype)


def _na_ctx(q, k, v, bsz, seq, n_ctx):
    d_att = q.shape[1]
    c0 = bsz * seq // n_ctx
    blk = pl.BlockSpec((n_ctx, LANES), lambda p, b: (c0 + b, p))
    return pl.pallas_call(
        functools.partial(_na_ctx_kernel, NA_HD ** -0.5),
        grid=(d_att // LANES, bsz),
        in_specs=[blk, blk, blk],
        out_specs=pl.BlockSpec((n_ctx, LANES), lambda p, b: (b, p)),
        out_shape=jax.ShapeDtypeStruct((bsz * n_ctx, d_att), BF16),
        compiler_params=_cparams(("arbitrary", "arbitrary")),
        name="na_ctx",
    )(q, k, v)


MOE_BLOCK = 256
COMBINE_TILE = 512
ROW_DMA_UNROLL = 8


def _row_dma_loop(n_rows, copy_of):
    def body(c, _):
        for u in range(ROW_DMA_UNROLL):
            copy_of(c * ROW_DMA_UNROLL + u).start()
        return 0

    lax.fori_loop(0, n_rows // ROW_DMA_UNROLL, body, 0)


def _expert_kernel(d_ff, be_ref, tok_ref, tokn_ref, rowp_ref, row_ref, f_hbm, wgu_ref, bgu_ref, wdn_ref, bdn_ref,
                   y_hbm, xbuf, ybuf, sem_in, sem_out, wgu_bf, wdn_bf, xbf):
    i = pl.program_id(0)
    n = pl.num_programs(0)
    slot = i % 2
    other = 1 - slot
    tpt = SUBLANES
    rows = xbuf.shape[1] // tpt

    def tile_of(idx_ref, j):
        return pl.ds(pl.multiple_of(idx_ref[0, j], tpt), tpt)

    def gather(idx_ref, s):
        return lambda j: pltpu.make_async_copy(f_hbm.at[tile_of(idx_ref, j), :], xbuf.at[s, pl.ds(j * tpt, tpt), :],
                                               sem_in.at[s])

    def scatter(idx_ref, s):
        return lambda j: pltpu.make_async_copy(ybuf.at[s, pl.ds(j * tpt, tpt), :], y_hbm.at[tile_of(idx_ref, j), :],
                                               sem_out.at[s])

    def wait_in(s):
        pltpu.make_async_copy(f_hbm.at[pl.ds(0, rows * tpt), :], xbuf.at[s], sem_in.at[s]).wait()

    def wait_out(s):
        pltpu.make_async_copy(ybuf.at[s], y_hbm.at[pl.ds(0, rows * tpt), :], sem_out.at[s]).wait()

    @pl.when(i == 0)
    def _():
        ybuf[...] = jnp.zeros_like(ybuf)
        _row_dma_loop(rows, gather(tok_ref, 0))

    wait_in(slot)

    @pl.when(i > 0)
    def _():
        wait_out(slot)

    @pl.when((i == 0) | (be_ref[i] != be_ref[jnp.maximum(i - 1, 0)]))
    def _():
        wgu_bf[...] = wgu_ref[...].astype(BF16)
        wdn_bf[...] = wdn_ref[...].astype(BF16)

    xbf[...] = _load_token_tiles(xbuf.at[slot], tpt).astype(BF16)
    for j in range(rows):
        gather(tokn_ref, other)(j).start()
        scatter(rowp_ref, other)(j).start()
    gu = jnp.dot(xbf[...], wgu_bf[...], preferred_element_type=F32) + bgu_ref[...]
    glu = jnp.minimum(gu[:, :d_ff], SWIGLU_LIMIT)
    lin = jnp.clip(gu[:, d_ff:], -SWIGLU_LIMIT, SWIGLU_LIMIT)
    act = glu * jax.nn.sigmoid(SWIGLU_ALPHA * glu) * (lin + 1.0)
    y = jnp.dot(act.astype(BF16), wdn_bf[...], preferred_element_type=F32) + bdn_ref[...]
    _store_token_tiles(ybuf.at[slot], y)

    @pl.when(i == n - 1)
    def _():
        _row_dma_loop(rows, scatter(row_ref, slot))
        wait_in(other)
        wait_out(other)
        wait_out(slot)


def _experts(f_all, blk_e, tok_buf, row_buf, n_out_rows, lyr, w_gu, b_gu, w_down, b_down):
    n_exp, d, d_gu = w_gu.shape[1:]
    assert d == SUBLANES * LANES
    d_ff = w_down.shape[2]
    nb = blk_e.shape[0]
    idspec = lambda f: pl.BlockSpec((None, 1, MOE_BLOCK), f, memory_space=pltpu.SMEM)
    wspec = lambda shp: pl.BlockSpec((None, None) + shp, lambda i, be: (lyr, be[i], 0, 0))
    return pl.pallas_call(
        functools.partial(_expert_kernel, d_ff),
        grid_spec=pltpu.PrefetchScalarGridSpec(
            num_scalar_prefetch=1,
            grid=(nb,),
            in_specs=[
                idspec(lambda i, be: (i, 0, 0)),
                idspec(lambda i, be: (jnp.minimum(i + 1, nb - 1), 0, 0)),
                idspec(lambda i, be: (jnp.where(i == 0, nb, i - 1), 0, 0)),
                idspec(lambda i, be: (i, 0, 0)),
                pl.BlockSpec(memory_space=pl.ANY),
                wspec((d, d_gu)), wspec((1, d_gu)), wspec((d_ff, d)), wspec((1, d)),
            ],
            out_specs=pl.BlockSpec(memory_space=pl.ANY),
            scratch_shapes=[pltpu.VMEM((2, MOE_BLOCK * SUBLANES, LANES), F32),
                            pltpu.VMEM((2, MOE_BLOCK * SUBLANES, LANES), F32),
                            pltpu.SemaphoreType.DMA((2,)), pltpu.SemaphoreType.DMA((2,)),
                            pltpu.VMEM((d, d_gu), BF16), pltpu.VMEM((d_ff, d), BF16),
                            pltpu.VMEM((MOE_BLOCK, d), BF16)],
        ),
        out_shape=jax.ShapeDtypeStruct((n_out_rows * SUBLANES, LANES), F32),
        compiler_params=_cparams(("arbitrary",)),
        name="experts",
    )(blk_e, tok_buf, tok_buf, row_buf, row_buf, f_all, w_gu, b_gu.reshape(-1, n_exp, 1, d_gu), w_down,
      b_down.reshape(-1, n_exp, 1, d))


def _combine_kernel(*refs):
    y_refs, (tg_ref, x_ref, mod_ref, g_ref, o_ref) = refs[:TOP_K], refs[TOP_K:]
    tg = tg_ref[...]
    y = None
    for k in range(TOP_K):
        yk = _load_token_tiles(y_refs[k], SUBLANES) * tg[:, k:k + 1]
        y = yk if y is None else y + yk
    o_ref[...] = x_ref[...] + mod_ref[5:6, :] * (_rms(y) * g_ref[3:4, :])


def _combine(y4, tg, x_new, modl, g, n_tok, bsz, seq):
    tm = COMBINE_TILE
    d = x_new.shape[1]
    nt = n_tok // tm
    mod_row, _ = _row_maps(bsz, seq, tm)
    row = lambda i: (i, 0)
    return pl.pallas_call(
        _combine_kernel,
        grid=(nt,),
        in_specs=[pl.BlockSpec((tm * SUBLANES, LANES), functools.partial(lambda k, i: (k * nt + i, 0), k))
                  for k in range(TOP_K)] + [
            pl.BlockSpec((tm, LANES), row),
            pl.BlockSpec((tm, d), row),
            pl.BlockSpec((None, N_MOD, d), lambda i: (mod_row(i), 0, 0)),
            pl.BlockSpec(g.shape, lambda i: (0, 0)),
        ],
        out_specs=pl.BlockSpec((tm, d), row),
        out_shape=jax.ShapeDtypeStruct((n_tok, d), F32),
        compiler_params=_cparams(("arbitrary",)),
        name="combine",
    )(*([y4] * TOP_K), tg, x_new, modl, g)


def _routing_tables(ti, n_tok, n_exp):
    e_flat = ti[:n_tok, :TOP_K].reshape(-1)
    n_assign = n_tok * TOP_K
    nb = -(-n_assign // MOE_BLOCK) + n_exp
    n_fill = nb * MOE_BLOCK - n_assign
    experts = jnp.arange(n_exp, dtype=jnp.int32)
    counts = (e_flat[:, None] == experts).sum(axis=0).astype(jnp.int32)
    pad_end = jnp.cumsum((-counts) % MOE_BLOCK)
    fill = jnp.arange(n_fill, dtype=jnp.int32)
    fill_e = (fill[:, None] >= pad_end[None, :]).sum(axis=1).astype(jnp.int32)
    keys = jnp.concatenate([2 * e_flat, 2 * fill_e + 1])
    ids = jnp.concatenate([jnp.arange(n_assign, dtype=jnp.int32), fill])
    id_bits = max(n_assign, n_fill).bit_length()
    assert (2 * n_exp + 2) << id_bits < 2 ** 31
    packed = jnp.sort((keys << id_bits) | ids)
    keys_s, ids_s = packed >> id_bits, packed & ((1 << id_bits) - 1)
    real = keys_s % 2 == 0
    tok_buf = jnp.where(real, ids_s // TOP_K, 0)
    spare0 = TOP_K * n_tok
    row_buf = jnp.where(real, (ids_s % TOP_K) * n_tok + ids_s // TOP_K, spare0 + ids_s)
    row_buf = jnp.concatenate([row_buf, spare0 + n_fill + jnp.arange(MOE_BLOCK, dtype=jnp.int32)])
    blk_e = jnp.minimum(keys_s.reshape(nb, MOE_BLOCK)[:, 0] // 2, n_exp - 1)
    n_out_rows = spare0 + n_fill + MOE_BLOCK
    tile0 = lambda t: (t * SUBLANES).reshape(-1, 1, MOE_BLOCK)
    return blk_e, tile0(tok_buf), tile0(row_buf), n_out_rows


def _moe(f_all, ti, tg, x_new, modl, g, lyr, w_gu, b_gu, w_down, b_down, n_tok, bsz, seq):
    n_exp = w_gu.shape[1]
    blk_e, tok_buf, row_buf, n_out_rows = _routing_tables(ti, n_tok, n_exp)
    y4 = _experts(f_all, blk_e, tok_buf, row_buf, n_out_rows, lyr, w_gu, b_gu, w_down, b_down)
    return _combine(y4, tg, x_new, modl, g, n_tok, bsz, seq)


def kernel(x, c, ctx, c_ctx, w_mod, b_mod, norm_g, w_in, w_out, lru_conv_w, lru_conv_b, lru_w_a, lru_b_a, lru_w_x,
           lru_b_x, lru_lambda, s5_a_re, s5_a_im, s5_log_dt, s5_b_re, s5_b_im, s5_c_re, s5_c_im, s5_d, s5_w_glu,
           s5_b_glu, na_rpb, moe_w_router, moe_b_router, moe_w_gu, moe_b_gu, moe_w_down, moe_b_down):
    bsz, seq, d = x.shape
    n_ctx = ctx.shape[1]
    depth = w_mod.shape[0]
    d_lru = lru_conv_w.shape[2]
    d_s5 = s5_d.shape[1]
    d_att = w_out.shape[1] - d_lru - d_s5
    n_exp = moe_w_router.shape[2]
    n_lat = bsz * seq
    assert bsz * n_ctx == ROW_TILE and bsz + 1 <= SUBLANES and n_exp <= LANES

    cvec = jnp.zeros((SUBLANES, d), F32).at[:bsz].set(c).at[bsz].set(c_ctx)
    mods = _modulation(cvec, w_mod, b_mod).reshape(depth, SUBLANES, N_MOD, d)
    cos_t, sin_t = _rope_tables(seq)
    x_all = jnp.concatenate([x.reshape(n_lat, d), ctx.reshape(bsz * n_ctx, d)], axis=0)

    for l in range(depth):
        need_ctx = l < depth - 1
        modl, g = mods[l], norm_g[l]
        lu, lg, su, q, qr, kr, v = _inproj(x_all, modl, g[0:1], w_in[l].astype(BF16), cos_t, sin_t,
                                           (d_lru, d_s5, d_att), bsz, seq)
        hs_l, hs_c = _lru_mixer(lu, bsz, seq, n_ctx, lru_conv_w[l], lru_conv_b[l], lru_w_a[l], lru_b_a[l],
                                lru_w_x[l], lru_b_x[l], lru_lambda[l])
        ys_l, ys_c = _s5_mixer(su, bsz, seq, n_ctx, s5_a_re[l], s5_a_im[l], s5_log_dt[l], s5_b_re[l], s5_b_im[l],
                               s5_c_re[l], s5_c_im[l], s5_d[l])
        na_l = _na_lat(q, qr, kr, v, bsz, seq, n_ctx, na_rpb[l])
        lat = (hs_l[0], hs_l[1], ys_l, na_l)
        cx = (hs_c[0], hs_c[1], ys_c, _na_ctx(q, kr, v, bsz, seq, n_ctx)) if need_ctx else None
        wr_pad = jnp.zeros((d, LANES), F32).at[:, :n_exp].set(moe_w_router[l])
        wr_hi = wr_pad.astype(BF16)
        wr_lo = (wr_pad - wr_hi.astype(F32)).astype(BF16)
        br_pad = jnp.zeros((1, LANES), F32).at[0, :n_exp].set(moe_b_router[l])
        x_new, f_all, ti, tg = _outproj(lat, cx, lg, x_all, modl, g, w_out[l].astype(BF16), s5_w_glu[l].astype(BF16),
                                        s5_b_glu[l][None, :], wr_hi, wr_lo, br_pad, bsz, seq, n_exp)
        n_tok = x_new.shape[0]
        x_all = _moe(f_all, ti, tg, x_new, modl, g, l, moe_w_gu, moe_b_gu, moe_w_down, moe_b_down, n_tok, bsz, seq)
    return x_all[:n_lat].reshape(bsz, seq, d)
```
